```python
import math
import jax, jax.numpy as jnp
from jax import lax
import numpy as np

D_MODEL = 4096
BATCH = 1
SEQ = 8192
DEPTH = 1
DEC_BATCH = 8
DEC_SEQ = 64
PAST_LEN = 2048

CHUNK = 64
Q_BLOCK = 128
ROPE_THETA = 500000.0
EPS = 1e-6
NEG = -1e30
H_A = 16
DH_A = 64
DV_A = 2 * DH_A
ROT_A = DH_A // 4
SCALE_A = DH_A ** -0.5
H_B = 16
Q_LORA = 896
KV_LORA = 512
D_NOPE = 128
D_ROPE = 64
DV_B = 128
SCALE_B = (D_NOPE + D_ROPE) ** -0.5
D_FF = 11008
CONV_W = 3
N_QK_A = 2 * H_A * DH_A
N_V_A = H_A * DV_A
N_KV_B = KV_LORA + D_ROPE
N_GATE = 2 * D_MODEL
SPLITS = [N_QK_A, 2 * N_QK_A, 2 * N_QK_A + N_V_A, 2 * N_QK_A + N_V_A + Q_LORA,
          2 * N_QK_A + N_V_A + Q_LORA + N_KV_B]
N_IN = SPLITS[-1] + N_GATE

kernel_name = 'gated_diffattn_mla_convffn_stream_step'


def _rmsnorm(x, g):
    xf = x.astype(jnp.float32)
    y = xf * lax.rsqrt(jnp.mean(xf * xf, axis=-1, keepdims=True) + EPS)
    return (y * g.astype(jnp.float32)).astype(x.dtype)


def _rope(x, pos, rot):
    half = rot // 2
    inv = jnp.power(jnp.float32(ROPE_THETA), -jnp.arange(half, dtype=jnp.float32) / half)
    ang = pos.astype(jnp.float32)[:, None] * inv[None, :]
    cos = jnp.cos(ang)[None, :, None, :]
    sin = jnp.sin(ang)[None, :, None, :]
    xr = x[..., :rot].astype(jnp.float32)
    x1, x2 = xr[..., :half], xr[..., half:]
    xr = jnp.concatenate([x1 * cos - x2 * sin, x2 * cos + x1 * sin], axis=-1).astype(x.dtype)
    return jnp.concatenate([xr, x[..., rot:]], axis=-1)


def _diff_attend(q1, q2, k1, k2, v, lam, mask):
    s1 = jnp.einsum('bqhd,bkhd->bhqk', q1, k1).astype(jnp.float32) * SCALE_A
    s2 = jnp.einsum('bqhd,bkhd->bhqk', q2, k2).astype(jnp.float32) * SCALE_A
    if mask is not None:
        s1 = jnp.where(mask, s1, NEG)
        s2 = jnp.where(mask, s2, NEG)
    p = jax.nn.softmax(s1, axis=-1) - lam * jax.nn.softmax(s2, axis=-1)
    return jnp.einsum('bhqk,bkhd->bqhd', p.astype(v.dtype), v)


def _mla_attend(q_nope, q_pe, k_nope, k_pe, v, mask):
    s = (jnp.einsum('bqhd,bkhd->bhqk', q_nope, k_nope)
         + jnp.einsum('bqhd,bkd->bhqk', q_pe, k_pe)).astype(jnp.float32) * SCALE_B
    if mask is not None:
        s = jnp.where(mask, s, NEG)
    p = jax.nn.softmax(s, axis=-1)
    return jnp.einsum('bhqk,bkhd->bqhd', p.astype(v.dtype), v)


def _sweep(fn, qs, T):
    nblk = T // Q_BLOCK
    qb = tuple(jnp.swapaxes(q.reshape(q.shape[0], nblk, Q_BLOCK, *q.shape[2:]), 0, 1) for q in qs)
    kchunk = jnp.arange(T) // CHUNK

    def body(args):
        i, blk = args
        qchunk = (i * Q_BLOCK + jnp.arange(Q_BLOCK)) // CHUNK
        mask = kchunk[None, :] <= qchunk[:, None]
        return fn(*blk, mask)

    out = lax.map(body, (jnp.arange(nblk), qb))
    out = jnp.swapaxes(out, 0, 1)
    return out.reshape(out.shape[0], T, *out.shape[3:])


def _layer(x, pos, past, lam_init, p):
    B, T, _ = x.shape
    xn = _rmsnorm(x, p['g_attn'])
    proj = xn @ p['w_in']
    q_a, k_a, v_a, c_q, c_kv, gate = jnp.split(proj, SPLITS, axis=-1)
    qa = _rope(q_a.reshape(B, T, 2 * H_A, DH_A), pos, ROT_A)
    ka = _rope(k_a.reshape(B, T, 2 * H_A, DH_A), pos, ROT_A)
    va = v_a.reshape(B, T, H_A, DV_A)
    qb = (_rmsnorm(c_q, p['g_qa']) @ p['w_qb']).reshape(B, T, H_B, D_NOPE + D_ROPE)
    qb_nope = qb[..., :D_NOPE]
    qb_pe = _rope(qb[..., D_NOPE:], pos, D_ROPE)
    ckv = _rmsnorm(c_kv[..., :KV_LORA], p['g_kva'])
    kpe = _rope(c_kv[..., None, KV_LORA:], pos, D_ROPE)[:, :, 0]
    if past is None:
        ka_all, va_all, ckv_all, kpe_all = ka, va, ckv, kpe
        conv_past = jnp.zeros((B, CONV_W - 1, D_FF), x.dtype)
    else:
        ka_all = jnp.concatenate([past['dk'], ka], axis=1)
        va_all = jnp.concatenate([past['dv'], va], axis=1)
        ckv_all = jnp.concatenate([past['ckv'], ckv], axis=1)
        kpe_all = jnp.concatenate([past['kpe'], kpe], axis=1)
        conv_past = past['conv']
    kvb = (ckv_all @ p['w_kvb']).reshape(B, -1, H_B, D_NOPE + DV_B)
    kb_nope, vb = kvb[..., :D_NOPE], kvb[..., D_NOPE:]
    f32 = jnp.float32
    lam = (jnp.exp(jnp.sum(p['lam_q1'].astype(f32) * p['lam_k1'].astype(f32)))
           - jnp.exp(jnp.sum(p['lam_q2'].astype(f32) * p['lam_k2'].astype(f32))) + lam_init)
    k1, k2 = ka_all[:, :, :H_A], ka_all[:, :, H_A:]
    q1, q2 = qa[:, :, :H_A], qa[:, :, H_A:]
    fa = lambda a1, a2, mask: _diff_attend(a1, a2, k1, k2, va_all, lam, mask)
    fb = lambda bn, bp, mask: _mla_attend(bn, bp, kb_nope, kpe_all, vb, mask)
    if past is None:
        oa = _sweep(fa, (q1, q2), T)
        ob = _sweep(fb, (qb_nope, qb_pe), T)
    else:
        oa = fa(q1, q2, None)
        ob = fb(qb_nope, qb_pe, None)
    oa = _rmsnorm(oa, p['g_subln']) * (1.0 - lam_init)
    ya = oa.reshape(B, T, H_A * DV_A) @ p['w_branch_a']
    yb = ob.reshape(B, T, H_B * DV_B) @ p['w_branch_b']
    g = jax.nn.sigmoid((gate + p['b_gate']).astype(f32)).astype(x.dtype).reshape(B, T, 2, D_MODEL)
    h = x + (g[:, :, 0] * ya + g[:, :, 1] * yb) @ p['w_out']
    hn = _rmsnorm(h, p['g_ffn'])
    u_g = hn @ p['w_ff_gate']
    u = hn @ p['w_ff_up']
    gp = jnp.concatenate([conv_past, u_g], axis=1)
    c = p['b_conv'] + sum(p['w_conv'][j] * gp[:, j:j + T] for j in range(CONV_W))
    y = h + (jax.nn.silu(c) * u) @ p['w_ff_down']
    return y, (ka, va, ckv, kpe, gp[:, T:])


def setup_inputs(seed: int = 0) -> dict:
    key = jax.random.key(seed)
    ks = jax.random.split(key, 40)
    f32 = jnp.float32
    nrm = lambda k, shape, s: jax.random.normal(k, shape, f32) * s
    gain = lambda k, n: 1.0 + 0.02 * jax.random.normal(k, (DEPTH, n), f32)
    return {
        'x_prompt': nrm(ks[0], (BATCH, SEQ, D_MODEL), 1.0),
        'x_sample': nrm(ks[1], (DEC_BATCH, DEC_SEQ, D_MODEL), 1.0),
        'cache_dk': nrm(ks[2], (DEPTH, DEC_BATCH, PAST_LEN, 2 * H_A, DH_A), 1.0),
        'cache_dv': nrm(ks[3], (DEPTH, DEC_BATCH, PAST_LEN, H_A, DV_A), 1.0),
        'cache_ckv': nrm(ks[4], (DEPTH, DEC_BATCH, PAST_LEN, KV_LORA), 1.0),
        'cache_kpe': nrm(ks[5], (DEPTH, DEC_BATCH, PAST_LEN, D_ROPE), 1.0),
        'state_conv': nrm(ks[6], (DEPTH, DEC_BATCH, CONV_W - 1, D_FF), 1.0),
        'g_attn': gain(ks[7], D_MODEL),
        'w_in': nrm(ks[8], (DEPTH, D_MODEL, N_IN), D_MODEL ** -0.5),
        'lam_q1': nrm(ks[9], (DEPTH, DH_A), 0.1),
        'lam_k1': nrm(ks[10], (DEPTH, DH_A), 0.1),
        'lam_q2': nrm(ks[11], (DEPTH, DH_A), 0.1),
        'lam_k2': nrm(ks[12], (DEPTH, DH_A), 0.1),
        'g_subln': gain(ks[13], DV_A),
        'w_branch_a': nrm(ks[14], (DEPTH, H_A * DV_A, D_MODEL), (H_A * DV_A) ** -0.5),
        'g_qa': gain(ks[15], Q_LORA),
        'w_qb': nrm(ks[16], (DEPTH, Q_LORA, H_B * (D_NOPE + D_ROPE)), Q_LORA ** -0.5),
        'g_kva': gain(ks[17], KV_LORA),
        'w_kvb': nrm(ks[18], (DEPTH, KV_LORA, H_B * (D_NOPE + DV_B)), KV_LORA ** -0.5),
        'w_branch_b': nrm(ks[19], (DEPTH, H_B * DV_B, D_MODEL), (H_B * DV_B) ** -0.5),
        'b_gate': nrm(ks[20], (DEPTH, N_GATE), 0.02),
        'w_out': nrm(ks[21], (DEPTH, D_MODEL, D_MODEL), D_MODEL ** -0.5),
        'g_ffn': gain(ks[22], D_MODEL),
        'w_ff_gate': nrm(ks[23], (DEPTH, D_MODEL, D_FF), D_MODEL ** -0.5),
        'w_ff_up': nrm(ks[24], (DEPTH, D_MODEL, D_FF), D_MODEL ** -0.5),
        'w_conv': nrm(ks[25], (DEPTH, CONV_W, D_FF), CONV_W ** -0.5),
        'b_conv': nrm(ks[26], (DEPTH, D_FF), 0.02),
        'w_ff_down': nrm(ks[27], (DEPTH, D_FF, D_MODEL), D_FF ** -0.5),
        'g_final': 1.0 + 0.02 * jax.random.normal(ks[28], (D_MODEL,), f32),
    }


def reference(x_prompt, x_sample, cache_dk, cache_dv, cache_ckv, cache_kpe, state_conv,
              g_attn, w_in, lam_q1, lam_k1, lam_q2, lam_k2, g_subln, w_branch_a,
              g_qa, w_qb, g_kva, w_kvb, w_branch_b, b_gate, w_out,
              g_ffn, w_ff_gate, w_ff_up, w_conv, b_conv, w_ff_down, g_final):
    T_p = x_prompt.shape[1]
    T_s = x_sample.shape[1]
    P = cache_dk.shape[2]
    pos_p = jnp.arange(T_p)
    pos_s = P + jnp.arange(T_s)
    hp, hs = x_prompt, x_sample
    outs_p, outs_s = [], []
    for l in range(DEPTH):
        lam_init = 0.8 - 0.6 * math.exp(-0.3 * l)
        p = dict(g_attn=g_attn[l], w_in=w_in[l], lam_q1=lam_q1[l], lam_k1=lam_k1[l],
                 lam_q2=lam_q2[l], lam_k2=lam_k2[l], g_subln=g_subln[l], w_branch_a=w_branch_a[l],
                 g_qa=g_qa[l], w_qb=w_qb[l], g_kva=g_kva[l], w_kvb=w_kvb[l], w_branch_b=w_branch_b[l],
                 b_gate=b_gate[l], w_out=w_out[l], g_ffn=g_ffn[l], w_ff_gate=w_ff_gate[l],
                 w_ff_up=w_ff_up[l], w_conv=w_conv[l], b_conv=b_conv[l], w_ff_down=w_ff_down[l])
        hp, st_p = _layer(hp, pos_p, None, lam_init, p)
        past = dict(dk=cache_dk[l], dv=cache_dv[l], ckv=cache_ckv[l], kpe=cache_kpe[l], conv=state_conv[l])
        hs, st_s = _layer(hs, pos_s, past, lam_init, p)
        outs_p.append(st_p)
        outs_s.append(st_s)
    stk = lambda outs, i: jnp.stack([o[i] for o in outs], axis=0)
    y_prompt = _rmsnorm(hp, g_final)
    y_sample = _rmsnorm(hs, g_final)
    return (y_prompt, y_sample,
            stk(outs_p, 0), stk(outs_p, 1), stk(outs_p, 2), stk(outs_p, 3), stk(outs_p, 4),
            stk(outs_s, 0), stk(outs_s, 1), stk(outs_s, 2), stk(outs_s, 3), stk(outs_s, 4))
```

```python
import functools
import math

import jax
import jax.numpy as jnp
from jax import lax
from jax.experimental import pallas as pl
from jax.experimental.pallas import tpu as pltpu

F32 = jnp.float32
BF16 = jnp.bfloat16

CHUNK = 64
ROPE_THETA = 500000.0
EPS = 1e-6
NEG = -1e30
H_A = 16
DH_A = 64
DV_A = 2 * DH_A
ROT_A = DH_A // 4
SCALE_A = DH_A ** -0.5
H_B = 16
Q_LORA = 896
KV_LORA = 512
D_NOPE = 128
D_ROPE = 64
DV_B = 128
SCALE_B = (D_NOPE + D_ROPE) ** -0.5
CONV_W = 3

LANES = 128
ROW_TILE = 512
VMEM_LIMIT = 56 * 1024 * 1024


def _params(sem):
    return pltpu.CompilerParams(dimension_semantics=sem, vmem_limit_bytes=VMEM_LIMIT)


def _rmsnorm_kernel(x_ref, g_ref, o_ref):
    x = x_ref[...].astype(F32)
    y = x * lax.rsqrt(jnp.mean(x * x, axis=-1, keepdims=True) + EPS)
    o_ref[...] = (y * g_ref[...]).astype(o_ref.dtype)


def _rmsnorm(x, g, out_dtype, tm=256):
    m, d = x.shape
    return pl.pallas_call(
        _rmsnorm_kernel,
        out_shape=jax.ShapeDtypeStruct((m, d), out_dtype),
        grid=(m // tm,),
        in_specs=[pl.BlockSpec((tm, d), lambda i: (i, 0)),
                  pl.BlockSpec((1, d), lambda i: (0, 0))],
        out_specs=pl.BlockSpec((tm, d), lambda i: (i, 0)),
        compiler_params=_params(("parallel",)),
        name="rmsnorm",
    )(x, g.reshape(1, d).astype(F32))


def _matmul(x, w, epilogue, *, tm, tn, tk, extras, out_shapes, out_blocks, name):
    m, kdim = x.shape
    n = w.shape[1]
    nk = kdim // tk
    n_ex, n_out = len(extras), len(out_shapes)

    def kern(x_ref, w_ref, *rest):
        ex, outs = rest[:n_ex], rest[n_ex:n_ex + n_out]
        xb = x_ref[...]
        if xb.dtype != BF16:
            xb = xb.astype(BF16)
        part = jnp.dot(xb, w_ref[...], preferred_element_type=F32)
        if nk == 1:
            epilogue(part, ex, outs)
            return
        acc = rest[-1]
        k = pl.program_id(2)

        @pl.when(k == 0)
        def _():
            acc[...] = part

        @pl.when(k > 0)
        def _():
            acc[...] += part

        @pl.when(k == nk - 1)
        def _():
            epilogue(acc[...], ex, outs)

    def lift(imap):
        return lambda i, j, k: imap(i, j)

    in_specs = [pl.BlockSpec((tm, tk), lambda i, j, k: (i, k)),
                pl.BlockSpec((tk, tn), lambda i, j, k: (k, j))]
    in_specs += [pl.BlockSpec(bs, lift(im)) for _, bs, im in extras]
    out_specs = [pl.BlockSpec(bs, lift(im)) for bs, im in out_blocks]
    return pl.pallas_call(
        kern,
        out_shape=out_shapes,
        grid=(m // tm, n // tn, nk),
        in_specs=in_specs,
        out_specs=out_specs,
        scratch_shapes=[pltpu.VMEM((tm, tn), F32)] if nk > 1 else [],
        compiler_params=_params(("parallel", "parallel", "arbitrary")),
        name=name,
    )(x, w, *[a for a, _, _ in extras])


def _rope(a, c_ref, sm_ref, sp_ref, half):
    outs = []
    for c in range(a.shape[1] // LANES):
        t = a[:, c * LANES:(c + 1) * LANES]
        nxt = pltpu.roll(t, LANES - half, axis=1)
        prv = pltpu.roll(t, half, axis=1)
        outs.append(t * c_ref[...] + nxt * sm_ref[...] + prv * sp_ref[...])
    return outs[0] if len(outs) == 1 else jnp.concatenate(outs, axis=1)


def _rope_tables(pos, rot, half):
    inv = jnp.power(jnp.float32(ROPE_THETA), -jnp.arange(half, dtype=F32) / half)
    ang = pos.astype(F32)[:, None] * inv[None, :]
    cos, sin = jnp.cos(ang), jnp.sin(ang)
    d = jnp.arange(LANES) % 64
    lo = d < half
    hi = (d >= half) & (d < rot)
    idx = jnp.where(lo, d, jnp.clip(d - half, 0, half - 1))
    cosg, sing = cos[:, idx], sin[:, idx]
    c = jnp.where((lo | hi)[None, :], cosg, 1.0)
    sm = jnp.where(lo[None, :], -sing, 0.0)
    sp = jnp.where(hi[None, :], sing, 0.0)
    return c.astype(F32), sm.astype(F32), sp.astype(F32)


def _softmax_step(s, v, idx, m_sc, l_sc, acc_sc):
    m_prev = m_sc[idx]
    m_new = jnp.maximum(m_prev, jnp.max(s, axis=1, keepdims=True))
    alpha = jnp.exp(m_prev - m_new)
    p = jnp.exp(s - m_new)
    l_sc[idx] = alpha * l_sc[idx] + jnp.sum(p, axis=1, keepdims=True)
    acc_sc[idx] = alpha * acc_sc[idx] + jnp.dot(p.astype(BF16), v, preferred_element_type=F32)
    m_sc[idx] = m_new


def _qk(q, k):
    return lax.dot_general(q, k, (((1,), (1,)), ((), ())), preferred_element_type=F32)


def _init_state(m_sc, l_sc, acc_sc):
    m_sc[...] = jnp.full(m_sc.shape, NEG, F32)
    l_sc[...] = jnp.zeros(l_sc.shape, F32)
    acc_sc[...] = jnp.zeros(acc_sc.shape, F32)


def _chunk_mask(t):
    row = lax.broadcasted_iota(jnp.int32, (t, t), 0) // CHUNK
    col = lax.broadcasted_iota(jnp.int32, (t, t), 1) // CHUNK
    return col <= row


def _split_heads(q, zero):
    lane = lax.broadcasted_iota(jnp.int32, q.shape, 1)
    return jnp.where(lane < DH_A, q, zero), jnp.where(lane >= DH_A, q, zero)


def _diff_finalize(lam_ref, g_ref, o_ref, l_sc, acc_sc, lam_init):
    lp = lam_ref[...]
    lam = (jnp.exp(jnp.sum(lp[0:1] * lp[1:2], axis=1, keepdims=True))
           - jnp.exp(jnp.sum(lp[2:3] * lp[3:4], axis=1, keepdims=True)) + lam_init)
    for e in range(2):
        o = acc_sc[2 * e] / l_sc[2 * e] - lam * (acc_sc[2 * e + 1] / l_sc[2 * e + 1])
        y = o * lax.rsqrt(jnp.mean(o * o, axis=-1, keepdims=True) + EPS)
        y = (y * g_ref[...]) * (1.0 - lam_init)
        o_ref[:, e * DV_A:(e + 1) * DV_A] = y.astype(o_ref.dtype)


def _attn_a_prompt_kernel(lam_ref, g_ref, q1_ref, q2_ref, k1_ref, k2_ref, v_ref, o_ref,
                          m_sc, l_sc, acc_sc, *, t, lam_init):
    i = pl.program_id(1)
    zero = jnp.zeros((t, LANES), BF16)
    q1 = _split_heads(q1_ref[...], zero)
    q2 = _split_heads(q2_ref[...], zero)
    _init_state(m_sc, l_sc, acc_sc)

    def step(kb, mask):
        off = pl.multiple_of(kb * t, t)
        k1 = k1_ref[pl.ds(off, t), :]
        k2 = k2_ref[pl.ds(off, t), :]
        v = v_ref[pl.ds(off, t), :]
        for e in range(2):
            ve = v[:, e * DV_A:(e + 1) * DV_A]
            for mp, (q, k) in enumerate(((q1[e], k1), (q2[e], k2))):
                s = _qk(q, k)
                if mask is not None:
                    s = jnp.where(mask, s, NEG)
                _softmax_step(s, ve, 2 * e + mp, m_sc, l_sc, acc_sc)

    def body(kb, carry):
        step(kb, None)
        return carry

    lax.fori_loop(0, i, body, 0)
    step(i, _chunk_mask(t))
    _diff_finalize(lam_ref, g_ref, o_ref, l_sc, acc_sc, lam_init)


def _attn_a_sample_kernel(lam_ref, g_ref, q1_ref, q2_ref, k1n_ref, k2n_ref, vn_ref,
                          k1p_ref, k2p_ref, vp_ref, o_ref, m_sc, l_sc, acc_sc, *, lam_init):
    ts = q1_ref.shape[0]
    zero = jnp.zeros((ts, LANES), BF16)
    q1 = _split_heads(q1_ref[...], zero)
    q2 = _split_heads(q2_ref[...], zero)
    _init_state(m_sc, l_sc, acc_sc)
    for k1_ref_, k2_ref_, v_ref_ in ((k1p_ref, k2p_ref, vp_ref), (k1n_ref, k2n_ref, vn_ref)):
        k1 = k1_ref_[...].reshape(k1_ref_.shape[-2:]).astype(BF16)
        k2 = k2_ref_[...].reshape(k2_ref_.shape[-2:]).astype(BF16)
        v = v_ref_[...].reshape(v_ref_.shape[-2:]).astype(BF16)
        for e in range(2):
            ve = v[:, e * DV_A:(e + 1) * DV_A]
            for mp, (q, k) in enumerate(((q1[e], k1), (q2[e], k2))):
                _softmax_step(_qk(q, k), ve, 2 * e + mp, m_sc, l_sc, acc_sc)
    _diff_finalize(lam_ref, g_ref, o_ref, l_sc, acc_sc, lam_init)


def _attn_b_prompt_kernel(qn_ref, qp_ref, kn_ref, kp_ref, v_ref, o_ref, m_sc, l_sc, acc_sc, *, t):
    i = pl.program_id(1)
    q = jnp.concatenate([qn_ref[...], qp_ref[...]], axis=1)
    _init_state(m_sc, l_sc, acc_sc)

    def step(kb, mask):
        off = pl.multiple_of(kb * t, t)
        k = jnp.concatenate([kn_ref[pl.ds(off, t), :], kp_ref[pl.ds(off, t), :]], axis=1)
        s = _qk(q, k)
        if mask is not None:
            s = jnp.where(mask, s, NEG)
        _softmax_step(s, v_ref[pl.ds(off, t), :], 0, m_sc, l_sc, acc_sc)

    def body(kb, carry):
        step(kb, None)
        return carry

    lax.fori_loop(0, i, body, 0)
    step(i, _chunk_mask(t))
    o_ref[...] = (acc_sc[0] / l_sc[0]).astype(o_ref.dtype)


def _attn_b_sample_kernel(qn_ref, qp_ref, knn_ref, kpn_ref, vn_ref, knp_ref, kpp_ref, vp_ref,
                          o_ref, m_sc, l_sc, acc_sc):
    q = jnp.concatenate([qn_ref[...], qp_ref[...]], axis=1)
    _init_state(m_sc, l_sc, acc_sc)
    for kn_ref, kp_ref, v_ref in ((knp_ref, kpp_ref, vp_ref), (knn_ref, kpn_ref, vn_ref)):
        kn = kn_ref[...].reshape(kn_ref.shape[-2:])
        kp = kp_ref[...].reshape(kp_ref.shape[-2:])
        v = v_ref[...].reshape(v_ref.shape[-2:])
        _softmax_step(_qk(q, jnp.concatenate([kn, kp], axis=1)), v, 0, m_sc, l_sc, acc_sc)
    o_ref[...] = (acc_sc[0] / l_sc[0]).astype(o_ref.dtype)


def _attn_scratch(n, t, dv):
    return [pltpu.VMEM((n, t, 1), F32), pltpu.VMEM((n, t, 1), F32), pltpu.VMEM((n, t, dv), F32)]


def _merge_kernel(oa_ref, ob_ref, wa_ref, wb_ref, g0_ref, g1_ref, o_ref):
    ya = jnp.dot(oa_ref[...], wa_ref[...], preferred_element_type=F32)
    yb = jnp.dot(ob_ref[...], wb_ref[...], preferred_element_type=F32)
    o_ref[...] = (g0_ref[...].astype(F32) * ya + g1_ref[...].astype(F32) * yb).astype(o_ref.dtype)


def _conv_gate(u, p0, p1, wc_ref, bc_ref):
    row = lax.broadcasted_iota(jnp.int32, u.shape, 0)
    s1 = jnp.where(row == 0, p1, pltpu.roll(u, 1, axis=0))
    s2 = jnp.where(row == 0, p0, jnp.where(row == 1, p1, pltpu.roll(u, 2, axis=0)))
    wc = wc_ref[...]
    return bc_ref[...] + ((wc[0:1] * s2 + wc[1:2] * s1) + wc[2:3] * u)


def _ffn_up_kernel(x_ref, wg_ref, wu_ref, wc_ref, bc_ref, st_ref, o_ref, tail_ref, carry_ref,
                   *, n_prompt_tiles, tn):
    i = pl.program_id(0)
    j = pl.program_id(1)
    x = x_ref[...]
    ug = jnp.dot(x, wg_ref[...], preferred_element_type=F32)
    uu = jnp.dot(x, wu_ref[...], preferred_element_type=F32)
    tm = ug.shape[0]
    nseg = tm // CHUNK
    col = pl.multiple_of(j * tn, tn)
    for s in range(nseg):
        tail_ref[0, 8 * s:8 * s + 8, :] = ug[CHUNK * s + CHUNK - 8:CHUNK * s + CHUNK]

    @pl.when(i < n_prompt_tiles)
    def _():
        prev = carry_ref[:, pl.ds(col, tn)]
        prev = jnp.where(i == 0, jnp.zeros_like(prev), prev)
        c = _conv_gate(ug, prev[6:7], prev[7:8], wc_ref, bc_ref)
        o_ref[...] = (jax.nn.silu(c) * uu).astype(o_ref.dtype)
        carry_ref[:, pl.ds(col, tn)] = ug[tm - 8:tm]

    @pl.when(i >= n_prompt_tiles)
    def _():
        st = st_ref[...]
        for s in range(nseg):
            seg = ug[CHUNK * s:CHUNK * (s + 1)]
            c = _conv_gate(seg, st[s, 0:1], st[s, 1:2], wc_ref, bc_ref)
            o_ref[CHUNK * s:CHUNK * (s + 1), :] = (
                jax.nn.silu(c) * uu[CHUNK * s:CHUNK * (s + 1)]).astype(o_ref.dtype)


def kernel(x_prompt, x_sample, cache_dk, cache_dv, cache_ckv, cache_kpe, state_conv, g_attn, w_in, lam_q1, lam_k1, lam_q2, lam_k2, g_subln, w_branch_a, g_qa, w_qb, g_kva, w_kvb, w_branch_b, b_gate, w_out, g_ffn, w_ff_gate, w_ff_up, w_conv, b_conv, w_ff_down, g_final):
    depth = w_in.shape[0]
    assert depth == 1
    lam_init = 0.8 - 0.6 * math.exp(-0.3 * 0)
    bp, tp, d = x_prompt.shape
    bs, ts, _ = x_sample.shape
    past = cache_dk.shape[2]
    dff = w_ff_gate.shape[2]
    assert bp == 1 and ts == CHUNK and tp % ROW_TILE == 0 and bs * ts == ROW_TILE
    mp_, ms_ = tp, bs * ts
    m = mp_ + ms_
    tm = ROW_TILE
    n_qk = 2 * H_A * DH_A
    n_va = H_A * DV_A
    o_cq = 2 * n_qk + n_va
    o_ckv = o_cq + Q_LORA
    o_gate = o_ckv + KV_LORA + D_ROPE

    w_in0 = w_in[0]
    w_q = w_in0[:, :n_qk].astype(BF16)
    w_k = w_in0[:, n_qk:2 * n_qk].astype(BF16)
    w_v = w_in0[:, 2 * n_qk:o_cq].astype(BF16)
    w_cq = w_in0[:, o_cq:o_ckv].astype(BF16)
    w_ckv = jnp.pad(w_in0[:, o_ckv:o_gate], ((0, 0), (0, LANES - D_ROPE))).astype(BF16)
    w_gate = w_in0[:, o_gate:].astype(BF16)
    wqb = w_qb[0].reshape(Q_LORA, H_B, D_NOPE + D_ROPE)
    w_qb_nope = wqb[:, :, :D_NOPE].reshape(Q_LORA, H_B * D_NOPE).astype(BF16)
    w_qb_pe = jnp.pad(wqb[:, :, D_NOPE:], ((0, 0), (0, 0), (0, LANES - D_ROPE))
                      ).reshape(Q_LORA, H_B * LANES).astype(BF16)
    w_kvb_p = w_kvb[0].reshape(KV_LORA, H_B, 2, D_NOPE).transpose(0, 2, 1, 3
                                                                  ).reshape(KV_LORA, 2 * H_B * D_NOPE).astype(BF16)
    w_a = w_branch_a[0].astype(BF16)
    w_b = w_branch_b[0].astype(BF16)
    w_o = w_out[0].astype(BF16)
    w_fg = w_ff_gate[0].astype(BF16)
    w_fu = w_ff_up[0].astype(BF16)
    w_fd = w_ff_down[0].astype(BF16)

    x_all = jnp.concatenate([x_prompt.reshape(mp_, d), x_sample.reshape(ms_, d)], axis=0)
    pos = jnp.concatenate([jnp.arange(tp), jnp.tile(past + jnp.arange(ts), bs)])
    tab_a = _rope_tables(pos, ROT_A, ROT_A // 2)
    tab_b = _rope_tables(pos, D_ROPE, D_ROPE // 2)
    tab_spec = (tm, LANES), lambda i, j: (i, 0)

    xn = _rmsnorm(x_all, g_attn[0], BF16)

    def ep_q(acc, ex, outs):
        outs[0][...] = (_rope(acc, *ex, ROT_A // 2) * SCALE_A).astype(BF16)

    def ep_k(acc, ex, outs):
        r = _rope(acc, *ex, ROT_A // 2)
        outs[0][...] = r
        outs[1][...] = r.astype(BF16)

    def ep_v(acc, ex, outs):
        outs[0][...] = acc
        outs[1][...] = acc.astype(BF16)

    tn = 512
    blk = (tm, tn), lambda i, j: (i, j)
    tabs_a = [(t_, *tab_spec) for t_ in tab_a]
    tabs_b = [(t_, *tab_spec) for t_ in tab_b]
    (q_a,) = _matmul(xn, w_q, ep_q, tm=tm, tn=tn, tk=d, extras=tabs_a,
                     out_shapes=[jax.ShapeDtypeStruct((m, n_qk), BF16)], out_blocks=[blk], name="proj_q")
    k_f, k_a = _matmul(xn, w_k, ep_k, tm=tm, tn=tn, tk=d, extras=tabs_a,
                       out_shapes=[jax.ShapeDtypeStruct((m, n_qk), F32),
                                   jax.ShapeDtypeStruct((m, n_qk), BF16)],
                       out_blocks=[blk, blk], name="proj_k")
    v_f, v_a = _matmul(xn, w_v, ep_v, tm=tm, tn=tn, tk=d, extras=[],
                       out_shapes=[jax.ShapeDtypeStruct((m, n_va), F32),
                                   jax.ShapeDtypeStruct((m, n_va), BF16)],
                       out_blocks=[blk, blk], name="proj_v")

    def ep_norm(acc, ex, outs):
        y = acc * lax.rsqrt(jnp.mean(acc * acc, axis=-1, keepdims=True) + EPS)
        outs[0][...] = (y * ex[0][...]).astype(BF16)

    (cqn,) = _matmul(xn, w_cq, ep_norm, tm=tm, tn=Q_LORA, tk=d,
                     extras=[(g_qa[0].reshape(1, Q_LORA), (1, Q_LORA), lambda i, j: (0, 0))],
                     out_shapes=[jax.ShapeDtypeStruct((m, Q_LORA), BF16)],
                     out_blocks=[((tm, Q_LORA), lambda i, j: (i, 0))], name="proj_cq")

    def ep_ckv(acc, ex, outs):
        c = acc[:, :KV_LORA]
        y = c * lax.rsqrt(jnp.mean(c * c, axis=-1, keepdims=True) + EPS) * ex[0][...]
        outs[0][...] = y
        outs[1][...] = y.astype(BF16)
        r = _rope(acc[:, KV_LORA:], *ex[1:], D_ROPE // 2)
        outs[2][...] = r
        outs[3][...] = r.astype(BF16)

    n_ckv = KV_LORA + LANES
    ckv_f, ckv_b, kpe_f, kpe_b = _matmul(
        xn, w_ckv, ep_ckv, tm=tm, tn=n_ckv, tk=d,
        extras=[(g_kva[0].reshape(1, KV_LORA), (1, KV_LORA), lambda i, j: (0, 0))] + tabs_b,
        out_shapes=[jax.ShapeDtypeStruct((m, KV_LORA), F32), jax.ShapeDtypeStruct((m, KV_LORA), BF16),
                    jax.ShapeDtypeStruct((m, LANES), F32), jax.ShapeDtypeStruct((m, LANES), BF16)],
        out_blocks=[((tm, KV_LORA), lambda i, j: (i, 0))] * 2 + [((tm, LANES), lambda i, j: (i, 0))] * 2,
        name="proj_ckv")

    def ep_gate(acc, ex, outs):
        outs[0][...] = jax.nn.sigmoid(acc + ex[0][...]).astype(BF16)

    (gate,) = _matmul(xn, w_gate, ep_gate, tm=tm, tn=tn, tk=d,
                      extras=[(b_gate[0].reshape(1, 2 * d), (1, tn), lambda i, j: (0, j))],
                      out_shapes=[jax.ShapeDtypeStruct((m, 2 * d), BF16)], out_blocks=[blk], name="proj_gate")

    def ep_scale(acc, ex, outs):
        outs[0][...] = (acc * SCALE_B).astype(BF16)

    def ep_rope_scale(acc, ex, outs):
        outs[0][...] = (_rope(acc, *ex, D_ROPE // 2) * SCALE_B).astype(BF16)

    def ep_cast(acc, ex, outs):
        outs[0][...] = acc.astype(BF16)

    nb = H_B * D_NOPE
    (qb_nope,) = _matmul(cqn, w_qb_nope, ep_scale, tm=tm, tn=tn, tk=Q_LORA, extras=[],
                         out_shapes=[jax.ShapeDtypeStruct((m, nb), BF16)], out_blocks=[blk], name="qb_nope")
    (qb_pe,) = _matmul(cqn, w_qb_pe, ep_rope_scale, tm=tm, tn=tn, tk=Q_LORA, extras=tabs_b,
                       out_shapes=[jax.ShapeDtypeStruct((m, nb), BF16)], out_blocks=[blk], name="qb_pe")
    (kvb,) = _matmul(ckv_b, w_kvb_p, ep_cast, tm=tm, tn=tn, tk=KV_LORA, extras=[],
                     out_shapes=[jax.ShapeDtypeStruct((m, 2 * nb), BF16)], out_blocks=[blk], name="kvb_new")
    ckv_past = cache_ckv[0].reshape(bs * past, KV_LORA)
    (kvb_past,) = _matmul(ckv_past, w_kvb_p, ep_cast, tm=tm, tn=tn, tk=KV_LORA, extras=[],
                          out_shapes=[jax.ShapeDtypeStruct((bs * past, 2 * nb), BF16)],
                          out_blocks=[blk], name="kvb_past")

    lam_p = jnp.stack([lam_q1[0], lam_k1[0], lam_q2[0], lam_k2[0]]).astype(F32)
    g_sub = g_subln[0].reshape(1, DV_A).astype(F32)
    npair = H_A // 2
    nq = tp // tm
    const2 = lambda *_: (0, 0)
    oa_p = pl.pallas_call(
        functools.partial(_attn_a_prompt_kernel, t=tm, lam_init=lam_init),
        out_shape=jax.ShapeDtypeStruct((mp_, n_va), BF16),
        grid=(npair, nq),
        in_specs=[pl.BlockSpec((4, DH_A), const2), pl.BlockSpec((1, DV_A), const2),
                  pl.BlockSpec((tm, LANES), lambda p, i: (i, p)),
                  pl.BlockSpec((tm, LANES), lambda p, i: (i, npair + p)),
                  pl.BlockSpec((tp, LANES), lambda p, i: (0, p)),
                  pl.BlockSpec((tp, LANES), lambda p, i: (0, npair + p)),
                  pl.BlockSpec((tp, 2 * DV_A), lambda p, i: (0, p))],
        out_specs=pl.BlockSpec((tm, 2 * DV_A), lambda p, i: (i, p)),
        scratch_shapes=_attn_scratch(4, tm, DV_A),
        compiler_params=_params(("parallel", "arbitrary")),
        name="attn_a_prompt",
    )(lam_p, g_sub, q_a, q_a, k_a, k_a, v_a)

    dk_past = cache_dk[0].reshape(bs, past, n_qk)
    dv_past = cache_dv[0].reshape(bs, past, n_va)
    r0 = mp_ // ts
    oa_s = pl.pallas_call(
        functools.partial(_attn_a_sample_kernel, lam_init=lam_init),
        out_shape=jax.ShapeDtypeStruct((ms_, n_va), BF16),
        grid=(bs, npair),
        in_specs=[pl.BlockSpec((4, DH_A), const2), pl.BlockSpec((1, DV_A), const2),
                  pl.BlockSpec((ts, LANES), lambda b, p: (r0 + b, p)),
                  pl.BlockSpec((ts, LANES), lambda b, p: (r0 + b, npair + p)),
                  pl.BlockSpec((ts, LANES), lambda b, p: (r0 + b, p)),
                  pl.BlockSpec((ts, LANES), lambda b, p: (r0 + b, npair + p)),
                  pl.BlockSpec((ts, 2 * DV_A), lambda b, p: (r0 + b, p)),
                  pl.BlockSpec((1, past, LANES), lambda b, p: (b, 0, p)),
                  pl.BlockSpec((1, past, LANES), lambda b, p: (b, 0, npair + p)),
                  pl.BlockSpec((1, past, 2 * DV_A), lambda b, p: (b, 0, p))],
        out_specs=pl.BlockSpec((ts, 2 * DV_A), lambda b, p: (b, p)),
        scratch_shapes=_attn_scratch(4, ts, DV_A),
        compiler_params=_params(("parallel", "arbitrary")),
        name="attn_a_sample",
    )(lam_p, g_sub, q_a, q_a, k_a, k_a, v_a, dk_past, dk_past, dv_past)

    ob_p = pl.pallas_call(
        functools.partial(_attn_b_prompt_kernel, t=tm),
        out_shape=jax.ShapeDtypeStruct((mp_, nb), BF16),
        grid=(H_B, nq),
        in_specs=[pl.BlockSpec((tm, D_NOPE), lambda h, i: (i, h)),
                  pl.BlockSpec((tm, LANES), lambda h, i: (i, h)),
                  pl.BlockSpec((tp, D_NOPE), lambda h, i: (0, h)),
                  pl.BlockSpec((tp, LANES), lambda h, i: (0, 0)),
                  pl.BlockSpec((tp, DV_B), lambda h, i: (0, H_B + h))],
        out_specs=pl.BlockSpec((tm, DV_B), lambda h, i: (i, h)),
        scratch_shapes=_attn_scratch(1, tm, DV_B),
        compiler_params=_params(("parallel", "arbitrary")),
        name="attn_b_prompt",
    )(qb_nope, qb_pe, kvb, kpe_b, kvb)

    kpe_past = jnp.pad(cache_kpe[0], ((0, 0), (0, 0), (0, LANES - D_ROPE))).astype(BF16)
    kvb_past3 = kvb_past.reshape(bs, past, 2 * nb)
    ob_s = pl.pallas_call(
        _attn_b_sample_kernel,
        out_shape=jax.ShapeDtypeStruct((ms_, nb), BF16),
        grid=(bs, H_B),
        in_specs=[pl.BlockSpec((ts, D_NOPE), lambda b, h: (r0 + b, h)),
                  pl.BlockSpec((ts, LANES), lambda b, h: (r0 + b, h)),
                  pl.BlockSpec((ts, D_NOPE), lambda b, h: (r0 + b, h)),
                  pl.BlockSpec((ts, LANES), lambda b, h: (r0 + b, 0)),
                  pl.BlockSpec((ts, DV_B), lambda b, h: (r0 + b, H_B + h)),
                  pl.BlockSpec((1, past, D_NOPE), lambda b, h: (b, 0, h)),
                  pl.BlockSpec((1, past, LANES), lambda b, h: (b, 0, 0)),
                  pl.BlockSpec((1, past, DV_B), lambda b, h: (b, 0, H_B + h))],
        out_specs=pl.BlockSpec((ts, DV_B), lambda b, h: (b, h)),
        scratch_shapes=_attn_scratch(1, ts, DV_B),
        compiler_params=_params(("parallel", "arbitrary")),
        name="attn_b_sample",
    )(qb_nope, qb_pe, kvb, kpe_b, kvb, kvb_past3, kpe_past, kvb_past3)

    oa = jnp.concatenate([oa_p, oa_s], axis=0)
    ob = jnp.concatenate([ob_p, ob_s], axis=0)

    ncol = d // tn
    merged = pl.pallas_call(
        _merge_kernel,
        out_shape=jax.ShapeDtypeStruct((m, d), BF16),
        grid=(m // tm, ncol),
        in_specs=[pl.BlockSpec((tm, n_va), lambda i, j: (i, 0)),
                  pl.BlockSpec((tm, nb), lambda i, j: (i, 0)),
                  pl.BlockSpec((n_va, tn), lambda i, j: (0, j)),
                  pl.BlockSpec((nb, tn), lambda i, j: (0, j)),
                  pl.BlockSpec((tm, tn), lambda i, j: (i, j)),
                  pl.BlockSpec((tm, tn), lambda i, j: (i, ncol + j))],
        out_specs=pl.BlockSpec((tm, tn), lambda i, j: (i, j)),
        compiler_params=_params(("parallel", "parallel")),
        name="branch_merge",
    )(oa, ob, w_a, w_b, gate, gate)

    def ep_resid(acc, ex, outs):
        outs[0][...] = ex[0][...] + acc

    (h,) = _matmul(merged, w_o, ep_resid, tm=tm, tn=tn, tk=d, extras=[(x_all, *blk)],
                   out_shapes=[jax.ShapeDtypeStruct((m, d), F32)], out_blocks=[blk], name="out_proj")

    hn = _rmsnorm(h, g_ffn[0], BF16)
    tnf = 256
    n_pt = mp_ // tm
    nseg = tm // CHUNK
    act, tails = pl.pallas_call(
        functools.partial(_ffn_up_kernel, n_prompt_tiles=n_pt, tn=tnf),
        out_shape=[jax.ShapeDtypeStruct((m, dff), BF16),
                   jax.ShapeDtypeStruct((m // tm, 8 * nseg, dff), F32)],
        grid=(m // tm, dff // tnf),
        in_specs=[pl.BlockSpec((tm, d), lambda i, j: (i, 0)),
                  pl.BlockSpec((d, tnf), lambda i, j: (0, j)),
                  pl.BlockSpec((d, tnf), lambda i, j: (0, j)),
                  pl.BlockSpec((CONV_W, tnf), lambda i, j: (0, j)),
                  pl.BlockSpec((1, tnf), lambda i, j: (0, j)),
                  pl.BlockSpec((bs, CONV_W - 1, tnf), lambda i, j: (0, 0, j))],
        out_specs=[pl.BlockSpec((tm, tnf), lambda i, j: (i, j)),
                   pl.BlockSpec((1, 8 * nseg, tnf), lambda i, j: (i, 0, j))],
        scratch_shapes=[pltpu.VMEM((8, dff), F32)],
        compiler_params=_params(("arbitrary", "arbitrary")),
        name="ffn_up",
    )(hn, w_fg, w_fu, w_conv[0], b_conv[0].reshape(1, dff), state_conv[0])

    tkd = dff // 2
    (y_pre,) = _matmul(act, w_fd, ep_resid, tm=tm, tn=tn, tk=tkd, extras=[(h, *blk)],
                       out_shapes=[jax.ShapeDtypeStruct((m, d), F32)], out_blocks=[blk], name="ffn_down")
    y = _rmsnorm(y_pre, g_final, F32)

    def split(a, tail):
        return a[:mp_].reshape(1, bp, tp, *tail), a[mp_:].reshape(1, bs, ts, *tail)

    dk_p, dk_s = split(k_f, (2 * H_A, DH_A))
    dv_p, dv_s = split(v_f, (H_A, DV_A))
    ckv_p, ckv_s = split(ckv_f, (KV_LORA,))
    kpe_p, kpe_s = split(kpe_f[:, :D_ROPE], (D_ROPE,))
    conv_p = tails[n_pt - 1, 8 * nseg - 2:].reshape(1, bp, CONV_W - 1, dff)
    conv_s = tails[n_pt].reshape(nseg, 8, dff)[:, 6:].reshape(1, bs, CONV_W - 1, dff)
    y_p = y[:mp_].reshape(bp, tp, d)
    y_s = y[mp_:].reshape(bs, ts, d)
    return (y_p, y_s, dk_p, dv_p, ckv_p, kpe_p, conv_p, dk_s, dv_s, ckv_s, kpe_s, conv_s)
```

```python
import functools
import math

import jax
import jax.numpy as jnp
from jax import lax
from jax.experimental import pallas as pl
from jax.experimental.pallas import tpu as pltpu

F32 = jnp.float32
BF16 = jnp.bfloat16

CHUNK = 64
ROPE_THETA = 500000.0
EPS = 1e-6
NEG = -1e30
H_A = 16
DH_A = 64
DV_A = 2 * DH_A
ROT_A = DH_A // 4
SCALE_A = DH_A ** -0.5
H_B = 16
Q_LORA = 896
KV_LORA = 512
D_NOPE = 128
D_ROPE = 64
DV_B = 128
SCALE_B = (D_NOPE + D_ROPE) ** -0.5
LOG2E = math.log2(math.e)
CONV_W = 3

LANES = 128
ROW_TILE = 512
VMEM_LIMIT = 56 * 1024 * 1024


def _params(sem):
    return pltpu.CompilerParams(dimension_semantics=sem, vmem_limit_bytes=VMEM_LIMIT)


def _rmsnorm_kernel(x_ref, g_ref, o_ref):
    x = x_ref[...].astype(F32)
    y = x * lax.rsqrt(jnp.mean(x * x, axis=-1, keepdims=True) + EPS)
    o_ref[...] = (y * g_ref[...]).astype(o_ref.dtype)


def _rmsnorm(x, g, out_dtype, tm=256):
    m, d = x.shape
    return pl.pallas_call(
        _rmsnorm_kernel,
        out_shape=jax.ShapeDtypeStruct((m, d), out_dtype),
        grid=(m // tm,),
        in_specs=[pl.BlockSpec((tm, d), lambda i: (i, 0)),
                  pl.BlockSpec((1, d), lambda i: (0, 0))],
        out_specs=pl.BlockSpec((tm, d), lambda i: (i, 0)),
        compiler_params=_params(("parallel",)),
        name="rmsnorm",
    )(x, g.reshape(1, d).astype(F32))


def _matmul(x, w, epilogue, *, tm, tn, tk, extras, out_shapes, out_blocks, name):
    m, kdim = x.shape
    n = w.shape[1]
    nk = kdim // tk
    n_ex, n_out = len(extras), len(out_shapes)

    def kern(x_ref, w_ref, *rest):
        ex, outs = rest[:n_ex], rest[n_ex:n_ex + n_out]
        xb = x_ref[...]
        if xb.dtype != BF16:
            xb = xb.astype(BF16)
        part = jnp.dot(xb, w_ref[...], preferred_element_type=F32)
        if nk == 1:
            epilogue(part, ex, outs)
            return
        acc = rest[-1]
        k = pl.program_id(2)

        @pl.when(k == 0)
        def _():
            acc[...] = part

        @pl.when(k > 0)
        def _():
            acc[...] += part

        @pl.when(k == nk - 1)
        def _():
            epilogue(acc[...], ex, outs)

    def lift(imap):
        return lambda i, j, k: imap(i, j)

    in_specs = [pl.BlockSpec((tm, tk), lambda i, j, k: (i, k)),
                pl.BlockSpec((tk, tn), lambda i, j, k: (k, j))]
    in_specs += [pl.BlockSpec(bs, lift(im)) for _, bs, im in extras]
    out_specs = [pl.BlockSpec(bs, lift(im)) for bs, im in out_blocks]
    return pl.pallas_call(
        kern,
        out_shape=out_shapes,
        grid=(m // tm, n // tn, nk),
        in_specs=in_specs,
        out_specs=out_specs,
        scratch_shapes=[pltpu.VMEM((tm, tn), F32)] if nk > 1 else [],
        compiler_params=_params(("parallel", "parallel", "arbitrary")),
        name=name,
    )(x, w, *[a for a, _, _ in extras])


def _rope(a, c_ref, sm_ref, sp_ref, half):
    outs = []
    for c in range(a.shape[1] // LANES):
        t = a[:, c * LANES:(c + 1) * LANES]
        nxt = pltpu.roll(t, LANES - half, axis=1)
        prv = pltpu.roll(t, half, axis=1)
        outs.append(t * c_ref[...] + nxt * sm_ref[...] + prv * sp_ref[...])
    return outs[0] if len(outs) == 1 else jnp.concatenate(outs, axis=1)


def _rope_tables(pos, rot, half):
    inv = jnp.power(jnp.float32(ROPE_THETA), -jnp.arange(half, dtype=F32) / half)
    ang = pos.astype(F32)[:, None] * inv[None, :]
    cos, sin = jnp.cos(ang), jnp.sin(ang)
    d = jnp.arange(LANES) % 64
    lo = d < half
    hi = (d >= half) & (d < rot)
    idx = jnp.where(lo, d, jnp.clip(d - half, 0, half - 1))
    cosg, sing = cos[:, idx], sin[:, idx]
    c = jnp.where((lo | hi)[None, :], cosg, 1.0)
    sm = jnp.where(lo[None, :], -sing, 0.0)
    sp = jnp.where(hi[None, :], sing, 0.0)
    return c.astype(F32), sm.astype(F32), sp.astype(F32)


def _softmax_step(s, v, idx, m_sc, l_sc, acc_sc):
    m_prev = m_sc[idx]
    m_new = jnp.maximum(m_prev, jnp.max(s, axis=1, keepdims=True))
    alpha = jnp.exp2(m_prev - m_new)
    p = jnp.exp2(s - m_new)
    l_sc[idx] = alpha * l_sc[idx] + jnp.sum(p, axis=1, keepdims=True)
    acc_sc[idx] = alpha * acc_sc[idx] + jnp.dot(p.astype(BF16), v, preferred_element_type=F32)
    m_sc[idx] = m_new


def _qk(q, k):
    return lax.dot_general(q, k, (((1,), (1,)), ((), ())), preferred_element_type=F32)


def _init_state(m_sc, l_sc, acc_sc):
    m_sc[...] = jnp.full(m_sc.shape, NEG, F32)
    l_sc[...] = jnp.zeros(l_sc.shape, F32)
    acc_sc[...] = jnp.zeros(acc_sc.shape, F32)


def _chunk_mask(t):
    row = lax.broadcasted_iota(jnp.int32, (t, t), 0) // CHUNK
    col = lax.broadcasted_iota(jnp.int32, (t, t), 1) // CHUNK
    return col <= row


def _split_heads(q, zero):
    lane = lax.broadcasted_iota(jnp.int32, q.shape, 1)
    return jnp.where(lane < DH_A, q, zero), jnp.where(lane >= DH_A, q, zero)


def _diff_finalize(lam_ref, g_ref, o_ref, l_sc, acc_sc, lam_init):
    lp = lam_ref[...]
    lam = (jnp.exp(jnp.sum(lp[0:1] * lp[1:2], axis=1, keepdims=True))
           - jnp.exp(jnp.sum(lp[2:3] * lp[3:4], axis=1, keepdims=True)) + lam_init)
    for e in range(2):
        o = acc_sc[2 * e] / l_sc[2 * e] - lam * (acc_sc[2 * e + 1] / l_sc[2 * e + 1])
        y = o * lax.rsqrt(jnp.mean(o * o, axis=-1, keepdims=True) + EPS)
        y = (y * g_ref[...]) * (1.0 - lam_init)
        o_ref[:, e * DV_A:(e + 1) * DV_A] = y.astype(o_ref.dtype)


def _chunk_mask_t(t):
    key = lax.broadcasted_iota(jnp.int32, (t, t), 0) // CHUNK
    qry = lax.broadcasted_iota(jnp.int32, (t, t), 1) // CHUNK
    return key <= qry


def _stage_scores(k, q, mask, s_ref):
    s = _qk(k, q)
    if mask is not None:
        s = jnp.where(mask, s, NEG)
    s_ref[...] = s
    return jnp.max(s, axis=0, keepdims=True)


def _stage_probs(state, mraw, s_ref, p_ref):
    m, l, _ = state
    m_new = jnp.maximum(m, mraw)
    alpha = jnp.exp2(m - m_new)
    p = jnp.exp2(s_ref[...] - m_new)
    p_ref[...] = p.astype(BF16)
    return m_new, alpha * l + jnp.sum(p, axis=0, keepdims=True), alpha


def _stage_values(vt, alpha, p_ref, acc_ref):
    acc_ref[...] = alpha * acc_ref[...] + jnp.dot(vt, p_ref[...], preferred_element_type=F32)


def _flash_pipeline(i, t, n, scores_fn, vt_fn, buf_a, buf_b, acc_refs, finalize):
    for c in range(n):
        buf_b[1][c][...] = jnp.zeros(buf_b[1][c].shape, BF16)
        acc_refs[c][...] = jnp.zeros(acc_refs[c].shape, F32)
    mask = _chunk_mask_t(t)
    mraw = tuple(_stage_scores(*scores_fn(c, i), mask, buf_a[0][c]) for c in range(n))
    state = tuple((jnp.full((1, t), NEG, F32), jnp.zeros((1, t), F32), jnp.ones((1, t), F32))
                  for _ in range(n))

    def values(state, kb, p_refs):
        for c in range(n):
            _stage_values(vt_fn(c, kb), state[c][2], p_refs[c], acc_refs[c])

    def probs(state, mraw, buf):
        return tuple(_stage_probs(state[c], mraw[c], buf[0][c], buf[1][c]) for c in range(n))

    def half(carry, kb, rd, wr):
        state, mraw, cur, prev = carry
        values(state, prev, wr[1])
        state = probs(state, mraw, rd)
        mraw_next = tuple(_stage_scores(*scores_fn(c, kb), None, wr[0][c]) for c in range(n))
        return state, mraw_next, kb, cur

    def body(j, carry):
        carry = half(carry, 2 * j, buf_a, buf_b)
        return half(carry, 2 * j + 1, buf_b, buf_a)

    carry = lax.fori_loop(0, i // 2, body, (state, mraw, i, i))

    def tail(carry, rd, wr):
        state, mraw, cur, prev = carry
        values(state, prev, wr[1])
        state = probs(state, mraw, rd)
        values(state, cur, rd[1])
        finalize(tuple(st[1] for st in state))

    @pl.when(i % 2 == 0)
    def _():
        tail(carry, buf_a, buf_b)

    @pl.when(i % 2 == 1)
    def _():
        tail(half(carry, i - 1, buf_a, buf_b), buf_b, buf_a)


def _attn_a_prompt_kernel(lam_ref, g_ref, q1_ref, q2_ref, k1_ref, k2_ref, vt_ref, o_ref,
                          *scratch, t, lam_init):
    i = pl.program_id(1)
    buf_a, buf_b, acc_refs = (scratch[0:4], scratch[8:12]), (scratch[4:8], scratch[12:16]), scratch[16:20]
    zero = jnp.zeros((t, LANES), BF16)
    q1 = _split_heads(q1_ref[...], zero)
    q2 = _split_heads(q2_ref[...], zero)
    qs = (q1[0], q2[0], q1[1], q2[1])
    k_refs = (k1_ref, k2_ref, k1_ref, k2_ref)

    def scores_fn(c, kb):
        return k_refs[c][pl.ds(pl.multiple_of(kb * t, t), t), :], qs[c]

    def vt_fn(c, kb):
        return vt_ref[0, kb, (c // 2) * DV_A:(c // 2 + 1) * DV_A, :]

    def finalize(ls):
        lp = lam_ref[...]
        lam = (jnp.exp(jnp.sum(lp[0:1] * lp[1:2], axis=1, keepdims=True))
               - jnp.exp(jnp.sum(lp[2:3] * lp[3:4], axis=1, keepdims=True)) + lam_init)
        for e in range(2):
            ot = acc_refs[2 * e][...] / ls[2 * e] - lam * (acc_refs[2 * e + 1][...] / ls[2 * e + 1])
            o = ot.T
            y = o * lax.rsqrt(jnp.mean(o * o, axis=-1, keepdims=True) + EPS)
            y = (y * g_ref[...]) * (1.0 - lam_init)
            o_ref[:, e * DV_A:(e + 1) * DV_A] = y.astype(o_ref.dtype)

    _flash_pipeline(i, t, 4, scores_fn, vt_fn, buf_a, buf_b, acc_refs, finalize)


def _attn_a_sample_kernel(lam_ref, g_ref, q1_ref, q2_ref, k1n_ref, k2n_ref, vn_ref,
                          k1p_ref, k2p_ref, vp_ref, o_ref, m_sc, l_sc, acc_sc, *, lam_init):
    ts = q1_ref.shape[0]
    zero = jnp.zeros((ts, LANES), BF16)
    q1 = _split_heads(q1_ref[...], zero)
    q2 = _split_heads(q2_ref[...], zero)
    _init_state(m_sc, l_sc, acc_sc)
    for k1_ref_, k2_ref_, v_ref_ in ((k1p_ref, k2p_ref, vp_ref), (k1n_ref, k2n_ref, vn_ref)):
        k1 = k1_ref_[...].reshape(k1_ref_.shape[-2:]).astype(BF16)
        k2 = k2_ref_[...].reshape(k2_ref_.shape[-2:]).astype(BF16)
        v = v_ref_[...].reshape(v_ref_.shape[-2:]).astype(BF16)
        for e in range(2):
            ve = v[:, e * DV_A:(e + 1) * DV_A]
            for mp, (q, k) in enumerate(((q1[e], k1), (q2[e], k2))):
                _softmax_step(_qk(q, k), ve, 2 * e + mp, m_sc, l_sc, acc_sc)
    _diff_finalize(lam_ref, g_ref, o_ref, l_sc, acc_sc, lam_init)


def _attn_b_prompt_kernel(qn_ref, qp_ref, kn_ref, kp_ref, vt_ref, o_ref, sa_ref, sb_ref, pa_ref, pb_ref,
                          acc, *, t):
    i = pl.program_id(1)
    q = jnp.concatenate([qn_ref[...], qp_ref[...]], axis=1)

    def scores_fn(c, kb):
        off = pl.multiple_of(kb * t, t)
        return jnp.concatenate([kn_ref[pl.ds(off, t), :], kp_ref[pl.ds(off, t), :]], axis=1), q

    def finalize(ls):
        o_ref[...] = (acc[...] / ls[0]).T.astype(o_ref.dtype)

    _flash_pipeline(i, t, 1, scores_fn, lambda c, kb: vt_ref[0, kb], ((sa_ref,), (pa_ref,)),
                    ((sb_ref,), (pb_ref,)), (acc,), finalize)


def _attn_b_sample_kernel(qn_ref, qp_ref, knn_ref, kpn_ref, vn_ref, knp_ref, kpp_ref, vp_ref,
                          o_ref, m_sc, l_sc, acc_sc):
    q = jnp.concatenate([qn_ref[...], qp_ref[...]], axis=1)
    _init_state(m_sc, l_sc, acc_sc)
    for kn_ref, kp_ref, v_ref in ((knp_ref, kpp_ref, vp_ref), (knn_ref, kpn_ref, vn_ref)):
        kn = kn_ref[...].reshape(kn_ref.shape[-2:])
        kp = kp_ref[...].reshape(kp_ref.shape[-2:])
        v = v_ref[...].reshape(v_ref.shape[-2:])
        _softmax_step(_qk(q, jnp.concatenate([kn, kp], axis=1)), v, 0, m_sc, l_sc, acc_sc)
    o_ref[...] = (acc_sc[0] / l_sc[0]).astype(o_ref.dtype)


def _kv_major_t(v, groups, nblk, t):
    w = v.shape[1] // groups
    return v.reshape(nblk, t, groups, w).transpose(2, 0, 3, 1)


def _attn_scratch(n, t, dv):
    return [pltpu.VMEM((n, t, 1), F32), pltpu.VMEM((n, t, 1), F32), pltpu.VMEM((n, t, dv), F32)]


def _merge_kernel(oa_ref, ob_ref, wa_ref, wb_ref, g0_ref, g1_ref, o_ref):
    ya = jnp.dot(oa_ref[...], wa_ref[...], preferred_element_type=F32)
    yb = jnp.dot(ob_ref[...], wb_ref[...], preferred_element_type=F32)
    o_ref[...] = (g0_ref[...].astype(F32) * ya + g1_ref[...].astype(F32) * yb).astype(o_ref.dtype)


def _conv_gate(u, p0, p1, wc_ref, bc_ref):
    row = lax.broadcasted_iota(jnp.int32, u.shape, 0)
    s1 = jnp.where(row == 0, p1, pltpu.roll(u, 1, axis=0))
    s2 = jnp.where(row == 0, p0, jnp.where(row == 1, p1, pltpu.roll(u, 2, axis=0)))
    wc = wc_ref[...]
    return bc_ref[...] + ((wc[0:1] * s2 + wc[1:2] * s1) + wc[2:3] * u)


def _ffn_up_kernel(x_ref, wg_ref, wu_ref, wc_ref, bc_ref, st_ref, o_ref, tail_ref, carry_ref,
                   *, n_prompt_tiles, tn):
    i = pl.program_id(0)
    j = pl.program_id(1)
    x = x_ref[...]
    ug = jnp.dot(x, wg_ref[...], preferred_element_type=F32)
    uu = jnp.dot(x, wu_ref[...], preferred_element_type=F32)
    tm = ug.shape[0]
    nseg = tm // CHUNK
    col = pl.multiple_of(j * tn, tn)
    for s in range(nseg):
        tail_ref[0, 8 * s:8 * s + 8, :] = ug[CHUNK * s + CHUNK - 8:CHUNK * s + CHUNK]

    @pl.when(i < n_prompt_tiles)
    def _():
        prev = carry_ref[:, pl.ds(col, tn)]
        prev = jnp.where(i == 0, jnp.zeros_like(prev), prev)
        c = _conv_gate(ug, prev[6:7], prev[7:8], wc_ref, bc_ref)
        o_ref[...] = (jax.nn.silu(c) * uu).astype(o_ref.dtype)
        carry_ref[:, pl.ds(col, tn)] = ug[tm - 8:tm]

    @pl.when(i >= n_prompt_tiles)
    def _():
        st = st_ref[...]
        for s in range(nseg):
            seg = ug[CHUNK * s:CHUNK * (s + 1)]
            c = _conv_gate(seg, st[s, 0:1], st[s, 1:2], wc_ref, bc_ref)
            o_ref[CHUNK * s:CHUNK * (s + 1), :] = (
                jax.nn.silu(c) * uu[CHUNK * s:CHUNK * (s + 1)]).astype(o_ref.dtype)


def kernel(x_prompt, x_sample, cache_dk, cache_dv, cache_ckv, cache_kpe, state_conv, g_attn, w_in, lam_q1, lam_k1, lam_q2, lam_k2, g_subln, w_branch_a, g_qa, w_qb, g_kva, w_kvb, w_branch_b, b_gate, w_out, g_ffn, w_ff_gate, w_ff_up, w_conv, b_conv, w_ff_down, g_final):
    depth = w_in.shape[0]
    assert depth == 1
    lam_init = 0.8 - 0.6 * math.exp(-0.3 * 0)
    bp, tp, d = x_prompt.shape
    bs, ts, _ = x_sample.shape
    past = cache_dk.shape[2]
    dff = w_ff_gate.shape[2]
    assert bp == 1 and ts == CHUNK and tp % ROW_TILE == 0 and bs * ts == ROW_TILE
    mp_, ms_ = tp, bs * ts
    m = mp_ + ms_
    tm = ROW_TILE
    n_qk = 2 * H_A * DH_A
    n_va = H_A * DV_A
    o_cq = 2 * n_qk + n_va
    o_ckv = o_cq + Q_LORA
    o_gate = o_ckv + KV_LORA + D_ROPE

    w_in0 = w_in[0]
    w_q = w_in0[:, :n_qk].astype(BF16)
    w_k = w_in0[:, n_qk:2 * n_qk].astype(BF16)
    w_v = w_in0[:, 2 * n_qk:o_cq].astype(BF16)
    w_cq = w_in0[:, o_cq:o_ckv].astype(BF16)
    w_ckv = jnp.pad(w_in0[:, o_ckv:o_gate], ((0, 0), (0, LANES - D_ROPE))).astype(BF16)
    w_gate = w_in0[:, o_gate:].astype(BF16)
    wqb = w_qb[0].reshape(Q_LORA, H_B, D_NOPE + D_ROPE)
    w_qb_nope = wqb[:, :, :D_NOPE].reshape(Q_LORA, H_B * D_NOPE).astype(BF16)
    w_qb_pe = jnp.pad(wqb[:, :, D_NOPE:], ((0, 0), (0, 0), (0, LANES - D_ROPE))
                      ).reshape(Q_LORA, H_B * LANES).astype(BF16)
    w_kvb_p = w_kvb[0].reshape(KV_LORA, H_B, 2, D_NOPE).transpose(0, 2, 1, 3
                                                                  ).reshape(KV_LORA, 2 * H_B * D_NOPE).astype(BF16)
    w_a = w_branch_a[0].astype(BF16)
    w_b = w_branch_b[0].astype(BF16)
    w_o = w_out[0].astype(BF16)
    w_fg = w_ff_gate[0].astype(BF16)
    w_fu = w_ff_up[0].astype(BF16)
    w_fd = w_ff_down[0].astype(BF16)

    x_all = jnp.concatenate([x_prompt.reshape(mp_, d), x_sample.reshape(ms_, d)], axis=0)
    pos = jnp.concatenate([jnp.arange(tp), jnp.tile(past + jnp.arange(ts), bs)])
    tab_a = _rope_tables(pos, ROT_A, ROT_A // 2)
    tab_b = _rope_tables(pos, D_ROPE, D_ROPE // 2)
    tab_spec = (tm, LANES), lambda i, j: (i, 0)

    xn = _rmsnorm(x_all, g_attn[0], BF16)

    def ep_q(acc, ex, outs):
        outs[0][...] = (_rope(acc, *ex, ROT_A // 2) * (SCALE_A * LOG2E)).astype(BF16)

    def ep_k(acc, ex, outs):
        r = _rope(acc, *ex, ROT_A // 2)
        outs[0][...] = r
        outs[1][...] = r.astype(BF16)

    def ep_v(acc, ex, outs):
        outs[0][...] = acc
        outs[1][...] = acc.astype(BF16)

    tn = 512
    blk = (tm, tn), lambda i, j: (i, j)
    tabs_a = [(t_, *tab_spec) for t_ in tab_a]
    tabs_b = [(t_, *tab_spec) for t_ in tab_b]
    (q_a,) = _matmul(xn, w_q, ep_q, tm=tm, tn=tn, tk=d, extras=tabs_a,
                     out_shapes=[jax.ShapeDtypeStruct((m, n_qk), BF16)], out_blocks=[blk], name="proj_q")
    k_f, k_a = _matmul(xn, w_k, ep_k, tm=tm, tn=tn, tk=d, extras=tabs_a,
                       out_shapes=[jax.ShapeDtypeStruct((m, n_qk), F32),
                                   jax.ShapeDtypeStruct((m, n_qk), BF16)],
                       out_blocks=[blk, blk], name="proj_k")
    v_f, v_a = _matmul(xn, w_v, ep_v, tm=tm, tn=tn, tk=d, extras=[],
                       out_shapes=[jax.ShapeDtypeStruct((m, n_va), F32),
                                   jax.ShapeDtypeStruct((m, n_va), BF16)],
                       out_blocks=[blk, blk], name="proj_v")

    def ep_norm(acc, ex, outs):
        y = acc * lax.rsqrt(jnp.mean(acc * acc, axis=-1, keepdims=True) + EPS)
        outs[0][...] = (y * ex[0][...]).astype(BF16)

    (cqn,) = _matmul(xn, w_cq, ep_norm, tm=tm, tn=Q_LORA, tk=d,
                     extras=[(g_qa[0].reshape(1, Q_LORA), (1, Q_LORA), lambda i, j: (0, 0))],
                     out_shapes=[jax.ShapeDtypeStruct((m, Q_LORA), BF16)],
                     out_blocks=[((tm, Q_LORA), lambda i, j: (i, 0))], name="proj_cq")

    def ep_ckv(acc, ex, outs):
        c = acc[:, :KV_LORA]
        y = c * lax.rsqrt(jnp.mean(c * c, axis=-1, keepdims=True) + EPS) * ex[0][...]
        outs[0][...] = y
        outs[1][...] = y.astype(BF16)
        r = _rope(acc[:, KV_LORA:], *ex[1:], D_ROPE // 2)
        outs[2][...] = r
        outs[3][...] = r.astype(BF16)

    n_ckv = KV_LORA + LANES
    ckv_f, ckv_b, kpe_f, kpe_b = _matmul(
        xn, w_ckv, ep_ckv, tm=tm, tn=n_ckv, tk=d,
        extras=[(g_kva[0].reshape(1, KV_LORA), (1, KV_LORA), lambda i, j: (0, 0))] + tabs_b,
        out_shapes=[jax.ShapeDtypeStruct((m, KV_LORA), F32), jax.ShapeDtypeStruct((m, KV_LORA), BF16),
                    jax.ShapeDtypeStruct((m, LANES), F32), jax.ShapeDtypeStruct((m, LANES), BF16)],
        out_blocks=[((tm, KV_LORA), lambda i, j: (i, 0))] * 2 + [((tm, LANES), lambda i, j: (i, 0))] * 2,
        name="proj_ckv")

    def ep_gate(acc, ex, outs):
        outs[0][...] = jax.nn.sigmoid(acc + ex[0][...]).astype(BF16)

    (gate,) = _matmul(xn, w_gate, ep_gate, tm=tm, tn=tn, tk=d,
                      extras=[(b_gate[0].reshape(1, 2 * d), (1, tn), lambda i, j: (0, j))],
                      out_shapes=[jax.ShapeDtypeStruct((m, 2 * d), BF16)], out_blocks=[blk], name="proj_gate")

    def ep_scale(acc, ex, outs):
        outs[0][...] = (acc * (SCALE_B * LOG2E)).astype(BF16)

    def ep_rope_scale(acc, ex, outs):
        outs[0][...] = (_rope(acc, *ex, D_ROPE // 2) * (SCALE_B * LOG2E)).astype(BF16)

    def ep_cast(acc, ex, outs):
        outs[0][...] = acc.astype(BF16)

    nb = H_B * D_NOPE
    (qb_nope,) = _matmul(cqn, w_qb_nope, ep_scale, tm=tm, tn=tn, tk=Q_LORA, extras=[],
                         out_shapes=[jax.ShapeDtypeStruct((m, nb), BF16)], out_blocks=[blk], name="qb_nope")
    (qb_pe,) = _matmul(cqn, w_qb_pe, ep_rope_scale, tm=tm, tn=tn, tk=Q_LORA, extras=tabs_b,
                       out_shapes=[jax.ShapeDtypeStruct((m, nb), BF16)], out_blocks=[blk], name="qb_pe")
    (kvb,) = _matmul(ckv_b, w_kvb_p, ep_cast, tm=tm, tn=tn, tk=KV_LORA, extras=[],
                     out_shapes=[jax.ShapeDtypeStruct((m, 2 * nb), BF16)], out_blocks=[blk], name="kvb_new")
    ckv_past = cache_ckv[0].reshape(bs * past, KV_LORA)
    (kvb_past,) = _matmul(ckv_past, w_kvb_p, ep_cast, tm=tm, tn=tn, tk=KV_LORA, extras=[],
                          out_shapes=[jax.ShapeDtypeStruct((bs * past, 2 * nb), BF16)],
                          out_blocks=[blk], name="kvb_past")

    lam_p = jnp.stack([lam_q1[0], lam_k1[0], lam_q2[0], lam_k2[0]]).astype(F32)
    g_sub = g_subln[0].reshape(1, DV_A).astype(F32)
    npair = H_A // 2
    nq = tp // tm
    const2 = lambda *_: (0, 0)
    oa_p = pl.pallas_call(
        functools.partial(_attn_a_prompt_kernel, t=tm, lam_init=lam_init),
        out_shape=jax.ShapeDtypeStruct((mp_, n_va), BF16),
        grid=(npair, nq),
        in_specs=[pl.BlockSpec((4, DH_A), const2), pl.BlockSpec((1, DV_A), const2),
                  pl.BlockSpec((tm, LANES), lambda p, i: (i, p)),
                  pl.BlockSpec((tm, LANES), lambda p, i: (i, npair + p)),
                  pl.BlockSpec((tp, LANES), lambda p, i: (0, p)),
                  pl.BlockSpec((tp, LANES), lambda p, i: (0, npair + p)),
                  pl.BlockSpec((1, nq, 2 * DV_A, tm), lambda p, i: (p, 0, 0, 0))],
        out_specs=pl.BlockSpec((tm, 2 * DV_A), lambda p, i: (i, p)),
        scratch_shapes=([pltpu.VMEM((tm, tm), F32)] * 8 + [pltpu.VMEM((tm, tm), BF16)] * 8
                        + [pltpu.VMEM((DV_A, tm), F32)] * 4),
        compiler_params=_params(("parallel", "arbitrary")),
        name="attn_a_prompt",
    )(lam_p, g_sub, q_a, q_a, k_a, k_a, _kv_major_t(v_a[:mp_], npair, nq, tm))

    dk_past = cache_dk[0].reshape(bs, past, n_qk)
    dv_past = cache_dv[0].reshape(bs, past, n_va)
    r0 = mp_ // ts
    oa_s = pl.pallas_call(
        functools.partial(_attn_a_sample_kernel, lam_init=lam_init),
        out_shape=jax.ShapeDtypeStruct((ms_, n_va), BF16),
        grid=(bs, npair),
        in_specs=[pl.BlockSpec((4, DH_A), const2), pl.BlockSpec((1, DV_A), const2),
                  pl.BlockSpec((ts, LANES), lambda b, p: (r0 + b, p)),
                  pl.BlockSpec((ts, LANES), lambda b, p: (r0 + b, npair + p)),
                  pl.BlockSpec((ts, LANES), lambda b, p: (r0 + b, p)),
                  pl.BlockSpec((ts, LANES), lambda b, p: (r0 + b, npair + p)),
                  pl.BlockSpec((ts, 2 * DV_A), lambda b, p: (r0 + b, p)),
                  pl.BlockSpec((1, past, LANES), lambda b, p: (b, 0, p)),
                  pl.BlockSpec((1, past, LANES), lambda b, p: (b, 0, npair + p)),
                  pl.BlockSpec((1, past, 2 * DV_A), lambda b, p: (b, 0, p))],
        out_specs=pl.BlockSpec((ts, 2 * DV_A), lambda b, p: (b, p)),
        scratch_shapes=_attn_scratch(4, ts, DV_A),
        compiler_params=_params(("parallel", "arbitrary")),
        name="attn_a_sample",
    )(lam_p, g_sub, q_a, q_a, k_a, k_a, v_a, dk_past, dk_past, dv_past)

    ob_p = pl.pallas_call(
        functools.partial(_attn_b_prompt_kernel, t=tm),
        out_shape=jax.ShapeDtypeStruct((mp_, nb), BF16),
        grid=(H_B, nq),
        in_specs=[pl.BlockSpec((tm, D_NOPE), lambda h, i: (i, h)),
                  pl.BlockSpec((tm, LANES), lambda h, i: (i, h)),
                  pl.BlockSpec((tp, D_NOPE), lambda h, i: (0, h)),
                  pl.BlockSpec((tp, LANES), lambda h, i: (0, 0)),
                  pl.BlockSpec((1, nq, DV_B, tm), lambda h, i: (h, 0, 0, 0))],
        out_specs=pl.BlockSpec((tm, DV_B), lambda h, i: (i, h)),
        scratch_shapes=[pltpu.VMEM((tm, tm), F32)] * 2 + [pltpu.VMEM((tm, tm), BF16)] * 2 + [
                        pltpu.VMEM((DV_B, tm), F32)],
        compiler_params=_params(("parallel", "arbitrary")),
        name="attn_b_prompt",
    )(qb_nope, qb_pe, kvb, kpe_b, _kv_major_t(kvb[:mp_, nb:], H_B, nq, tm))

    kpe_past = jnp.pad(cache_kpe[0], ((0, 0), (0, 0), (0, LANES - D_ROPE))).astype(BF16)
    kvb_past3 = kvb_past.reshape(bs, past, 2 * nb)
    ob_s = pl.pallas_call(
        _attn_b_sample_kernel,
        out_shape=jax.ShapeDtypeStruct((ms_, nb), BF16),
        grid=(bs, H_B),
        in_specs=[pl.BlockSpec((ts, D_NOPE), lambda b, h: (r0 + b, h)),
                  pl.BlockSpec((ts, LANES), lambda b, h: (r0 + b, h)),
                  pl.BlockSpec((ts, D_NOPE), lambda b, h: (r0 + b, h)),
                  pl.BlockSpec((ts, LANES), lambda b, h: (r0 + b, 0)),
                  pl.BlockSpec((ts, DV_B), lambda b, h: (r0 + b, H_B + h)),
                  pl.BlockSpec((1, past, D_NOPE), lambda b, h: (b, 0, h)),
                  pl.BlockSpec((1, past, LANES), lambda b, h: (b, 0, 0)),
                  pl.BlockSpec((1, past, DV_B), lambda b, h: (b, 0, H_B + h))],
        out_specs=pl.BlockSpec((ts, DV_B), lambda b, h: (b, h)),
        scratch_shapes=_attn_scratch(1, ts, DV_B),
        compiler_params=_params(("parallel", "arbitrary")),
        name="attn_b_sample",
    )(qb_nope, qb_pe, kvb, kpe_b, kvb, kvb_past3, kpe_past, kvb_past3)

    oa = jnp.concatenate([oa_p, oa_s], axis=0)
    ob = jnp.concatenate([ob_p, ob_s], axis=0)

    ncol = d // tn
    merged = pl.pallas_call(
        _merge_kernel,
        out_shape=jax.ShapeDtypeStruct((m, d), BF16),
        grid=(m // tm, ncol),
        in_specs=[pl.BlockSpec((tm, n_va), lambda i, j: (i, 0)),
                  pl.BlockSpec((tm, nb), lambda i, j: (i, 0)),
                  pl.BlockSpec((n_va, tn), lambda i, j: (0, j)),
                  pl.BlockSpec((nb, tn), lambda i, j: (0, j)),
                  pl.BlockSpec((tm, tn), lambda i, j: (i, j)),
                  pl.BlockSpec((tm, tn), lambda i, j: (i, ncol + j))],
        out_specs=pl.BlockSpec((tm, tn), lambda i, j: (i, j)),
        compiler_params=_params(("parallel", "parallel")),
        name="branch_merge",
    )(oa, ob, w_a, w_b, gate, gate)

    def ep_resid(acc, ex, outs):
        outs[0][...] = ex[0][...] + acc

    (h,) = _matmul(merged, w_o, ep_resid, tm=tm, tn=tn, tk=d, extras=[(x_all, *blk)],
                   out_shapes=[jax.ShapeDtypeStruct((m, d), F32)], out_blocks=[blk], name="out_proj")

    hn = _rmsnorm(h, g_ffn[0], BF16)
    tnf = 256
    n_pt = mp_ // tm
    nseg = tm // CHUNK
    act, tails = pl.pallas_call(
        functools.partial(_ffn_up_kernel, n_prompt_tiles=n_pt, tn=tnf),
        out_shape=[jax.ShapeDtypeStruct((m, dff), BF16),
                   jax.ShapeDtypeStruct((m // tm, 8 * nseg, dff), F32)],
        grid=(m // tm, dff // tnf),
        in_specs=[pl.BlockSpec((tm, d), lambda i, j: (i, 0)),
                  pl.BlockSpec((d, tnf), lambda i, j: (0, j)),
                  pl.BlockSpec((d, tnf), lambda i, j: (0, j)),
                  pl.BlockSpec((CONV_W, tnf), lambda i, j: (0, j)),
                  pl.BlockSpec((1, tnf), lambda i, j: (0, j)),
                  pl.BlockSpec((bs, CONV_W - 1, tnf), lambda i, j: (0, 0, j))],
        out_specs=[pl.BlockSpec((tm, tnf), lambda i, j: (i, j)),
                   pl.BlockSpec((1, 8 * nseg, tnf), lambda i, j: (i, 0, j))],
        scratch_shapes=[pltpu.VMEM((8, dff), F32)],
        compiler_params=_params(("arbitrary", "arbitrary")),
        name="ffn_up",
    )(hn, w_fg, w_fu, w_conv[0], b_conv[0].reshape(1, dff), state_conv[0])

    tkd = dff // 2
    (y_pre,) = _matmul(act, w_fd, ep_resid, tm=tm, tn=tn, tk=tkd, extras=[(h, *blk)],
                       out_shapes=[jax.ShapeDtypeStruct((m, d), F32)], out_blocks=[blk], name="ffn_down")
    y = _rmsnorm(y_pre, g_final, F32)

    def split(a, tail):
        return a[:mp_].reshape(1, bp, tp, *tail), a[mp_:].reshape(1, bs, ts, *tail)

    dk_p, dk_s = split(k_f, (2 * H_A, DH_A))
    dv_p, dv_s = split(v_f, (H_A, DV_A))
    ckv_p, ckv_s = split(ckv_f, (KV_LORA,))
    kpe_p, kpe_s = split(kpe_f[:, :D_ROPE], (D_ROPE,))
    conv_p = tails[n_pt - 1, 8 * nseg - 2:].reshape(1, bp, CONV_W - 1, dff)
    conv_s = tails[n_pt].reshape(nseg, 8, dff)[:, 6:].reshape(1, bs, CONV_W - 1, dff)
    y_p = y[:mp_].reshape(bp, tp, d)
    y_s = y[mp_:].reshape(bs, ts, d)
    return (y_p, y_s, dk_p, dv_p, ckv_p, kpe_p, conv_p, dk_s, dv_s, ckv_s, kpe_s, conv_s)
```

```python
import functools
import math

import jax
import jax.numpy as jnp
from jax import lax
from jax.experimental import pallas as pl
from jax.experimental.pallas import tpu as pltpu

F32 = jnp.float32
BF16 = jnp.bfloat16

CHUNK = 64
ROPE_THETA = 500000.0
EPS = 1e-6
NEG = -1e30
H_A = 16
DH_A = 64
DV_A = 2 * DH_A
ROT_A = DH_A // 4
SCALE_A = DH_A ** -0.5
H_B = 16
Q_LORA = 896
KV_LORA = 512
D_NOPE = 128
D_ROPE = 64
DV_B = 128
SCALE_B = (D_NOPE + D_ROPE) ** -0.5
LOG2E = math.log2(math.e)
CONV_W = 3

LANES = 128
ROW_TILE = 512
VMEM_LIMIT = 56 * 1024 * 1024


def _params(sem):
    return pltpu.CompilerParams(dimension_semantics=sem, vmem_limit_bytes=VMEM_LIMIT)


def _rmsnorm_kernel(x_ref, g_ref, o_ref):
    x = x_ref[...].astype(F32)
    y = x * lax.rsqrt(jnp.mean(x * x, axis=-1, keepdims=True) + EPS)
    o_ref[...] = (y * g_ref[...]).astype(o_ref.dtype)


def _rmsnorm(x, g, out_dtype, tm=256):
    m, d = x.shape
    return pl.pallas_call(
        _rmsnorm_kernel,
        out_shape=jax.ShapeDtypeStruct((m, d), out_dtype),
        grid=(m // tm,),
        in_specs=[pl.BlockSpec((tm, d), lambda i: (i, 0)),
                  pl.BlockSpec((1, d), lambda i: (0, 0))],
        out_specs=pl.BlockSpec((tm, d), lambda i: (i, 0)),
        compiler_params=_params(("parallel",)),
        name="rmsnorm",
    )(x, g.reshape(1, d).astype(F32))


def _matmul(x, w, epilogue, *, tm, tn, tk, extras, out_shapes, out_blocks, name):
    m, kdim = x.shape
    n = w.shape[1]
    nk = kdim // tk
    n_ex, n_out = len(extras), len(out_shapes)

    def kern(x_ref, w_ref, *rest):
        ex, outs = rest[:n_ex], rest[n_ex:n_ex + n_out]
        xb = x_ref[...]
        if xb.dtype != BF16:
            xb = xb.astype(BF16)
        part = jnp.dot(xb, w_ref[...], preferred_element_type=F32)
        if nk == 1:
            epilogue(part, ex, outs)
            return
        acc = rest[-1]
        k = pl.program_id(2)

        @pl.when(k == 0)
        def _():
            acc[...] = part

        @pl.when(k > 0)
        def _():
            acc[...] += part

        @pl.when(k == nk - 1)
        def _():
            epilogue(acc[...], ex, outs)

    def lift(imap):
        return lambda i, j, k: imap(i, j)

    in_specs = [pl.BlockSpec((tm, tk), lambda i, j, k: (i, k)),
                pl.BlockSpec((tk, tn), lambda i, j, k: (k, j))]
    in_specs += [pl.BlockSpec(bs, lift(im)) for _, bs, im in extras]
    out_specs = [pl.BlockSpec(bs, lift(im)) for bs, im in out_blocks]
    return pl.pallas_call(
        kern,
        out_shape=out_shapes,
        grid=(m // tm, n // tn, nk),
        in_specs=in_specs,
        out_specs=out_specs,
        scratch_shapes=[pltpu.VMEM((tm, tn), F32)] if nk > 1 else [],
        compiler_params=_params(("parallel", "parallel", "arbitrary")),
        name=name,
    )(x, w, *[a for a, _, _ in extras])


def _rope(a, c_ref, sm_ref, sp_ref, half):
    outs = []
    for c in range(a.shape[1] // LANES):
        t = a[:, c * LANES:(c + 1) * LANES]
        nxt = pltpu.roll(t, LANES - half, axis=1)
        prv = pltpu.roll(t, half, axis=1)
        outs.append(t * c_ref[...] + nxt * sm_ref[...] + prv * sp_ref[...])
    return outs[0] if len(outs) == 1 else jnp.concatenate(outs, axis=1)


def _rope_tables(pos, rot, half):
    inv = jnp.power(jnp.float32(ROPE_THETA), -jnp.arange(half, dtype=F32) / half)
    ang = pos.astype(F32)[:, None] * inv[None, :]
    cos, sin = jnp.cos(ang), jnp.sin(ang)
    d = jnp.arange(LANES) % 64
    lo = d < half
    hi = (d >= half) & (d < rot)
    idx = jnp.where(lo, d, jnp.clip(d - half, 0, half - 1))
    cosg, sing = cos[:, idx], sin[:, idx]
    c = jnp.where((lo | hi)[None, :], cosg, 1.0)
    sm = jnp.where(lo[None, :], -sing, 0.0)
    sp = jnp.where(hi[None, :], sing, 0.0)
    return c.astype(F32), sm.astype(F32), sp.astype(F32)


def _softmax_step(s, v, idx, m_sc, l_sc, acc_sc):
    m_prev = m_sc[idx]
    m_new = jnp.maximum(m_prev, jnp.max(s, axis=1, keepdims=True))
    alpha = jnp.exp2(m_prev - m_new)
    p = jnp.exp2(s - m_new)
    l_sc[idx] = alpha * l_sc[idx] + jnp.sum(p, axis=1, keepdims=True)
    acc_sc[idx] = alpha * acc_sc[idx] + jnp.dot(p.astype(BF16), v, preferred_element_type=F32)
    m_sc[idx] = m_new


def _qk(q, k):
    return lax.dot_general(q, k, (((1,), (1,)), ((), ())), preferred_element_type=F32)


def _init_state(m_sc, l_sc, acc_sc):
    m_sc[...] = jnp.full(m_sc.shape, NEG, F32)
    l_sc[...] = jnp.zeros(l_sc.shape, F32)
    acc_sc[...] = jnp.zeros(acc_sc.shape, F32)


def _chunk_mask(t):
    row = lax.broadcasted_iota(jnp.int32, (t, t), 0) // CHUNK
    col = lax.broadcasted_iota(jnp.int32, (t, t), 1) // CHUNK
    return col <= row


def _split_heads(q, zero):
    lane = lax.broadcasted_iota(jnp.int32, q.shape, 1)
    return jnp.where(lane < DH_A, q, zero), jnp.where(lane >= DH_A, q, zero)


def _diff_finalize(lam_ref, g_ref, o_ref, l_sc, acc_sc, lam_init):
    lp = lam_ref[...]
    lam = (jnp.exp(jnp.sum(lp[0:1] * lp[1:2], axis=1, keepdims=True))
           - jnp.exp(jnp.sum(lp[2:3] * lp[3:4], axis=1, keepdims=True)) + lam_init)
    for e in range(2):
        o = acc_sc[2 * e] / l_sc[2 * e] - lam * (acc_sc[2 * e + 1] / l_sc[2 * e + 1])
        y = o * lax.rsqrt(jnp.mean(o * o, axis=-1, keepdims=True) + EPS)
        y = (y * g_ref[...]) * (1.0 - lam_init)
        o_ref[:, e * DV_A:(e + 1) * DV_A] = y.astype(o_ref.dtype)


def _chunk_mask_t(t):
    key = lax.broadcasted_iota(jnp.int32, (t, t), 0) // CHUNK
    qry = lax.broadcasted_iota(jnp.int32, (t, t), 1) // CHUNK
    return key <= qry


def _stage_scores(k, q, mask, s_ref):
    s = _qk(k, q)
    if mask is not None:
        s = jnp.where(mask, s, NEG)
    s_ref[...] = s
    return jnp.max(s, axis=0, keepdims=True)


def _stage_probs(state, mraw, s_ref, p_ref):
    m, l, _ = state
    m_new = jnp.maximum(m, mraw)
    alpha = jnp.exp2(m - m_new)
    p = jnp.exp2(s_ref[...] - m_new)
    p_ref[...] = p.astype(BF16)
    return m_new, alpha * l + jnp.sum(p, axis=0, keepdims=True), alpha


def _stage_values(vt, alpha, p_ref, acc_ref):
    acc_ref[...] = alpha * acc_ref[...] + jnp.dot(vt, p_ref[...], preferred_element_type=F32)


def _flash_pipeline(i, t, n, scores_fn, vt_fn, buf_a, buf_b, acc_refs, finalize):
    for c in range(n):
        buf_b[1][c][...] = jnp.zeros(buf_b[1][c].shape, BF16)
        acc_refs[c][...] = jnp.zeros(acc_refs[c].shape, F32)
    mask = _chunk_mask_t(t)
    mraw = tuple(_stage_scores(*scores_fn(c, i), mask, buf_a[0][c]) for c in range(n))
    state = tuple((jnp.full((1, t), NEG, F32), jnp.zeros((1, t), F32), jnp.ones((1, t), F32))
                  for _ in range(n))

    def values(state, kb, p_refs):
        for c in range(n):
            _stage_values(vt_fn(c, kb), state[c][2], p_refs[c], acc_refs[c])

    def probs(state, mraw, buf):
        return tuple(_stage_probs(state[c], mraw[c], buf[0][c], buf[1][c]) for c in range(n))

    def half(carry, kb, rd, wr):
        state, mraw, cur, prev = carry
        values(state, prev, wr[1])
        state = probs(state, mraw, rd)
        mraw_next = tuple(_stage_scores(*scores_fn(c, kb), None, wr[0][c]) for c in range(n))
        return state, mraw_next, kb, cur

    def body(j, carry):
        carry = half(carry, 2 * j, buf_a, buf_b)
        return half(carry, 2 * j + 1, buf_b, buf_a)

    carry = lax.fori_loop(0, i // 2, body, (state, mraw, i, i))

    def tail(carry, rd, wr):
        state, mraw, cur, prev = carry
        values(state, prev, wr[1])
        state = probs(state, mraw, rd)
        values(state, cur, rd[1])
        finalize(tuple(st[1] for st in state))

    @pl.when(i % 2 == 0)
    def _():
        tail(carry, buf_a, buf_b)

    @pl.when(i % 2 == 1)
    def _():
        tail(half(carry, i - 1, buf_a, buf_b), buf_b, buf_a)


def _attn_a_prompt_kernel(lam_ref, g_ref, q1_ref, q2_ref, k1_ref, k2_ref, vt_ref, o_ref,
                          *scratch, t, lam_init):
    i = pl.program_id(1)
    buf_a, buf_b, acc_refs = (scratch[0:4], scratch[8:12]), (scratch[4:8], scratch[12:16]), scratch[16:20]
    zero = jnp.zeros((t, LANES), BF16)
    q1 = _split_heads(q1_ref[...], zero)
    q2 = _split_heads(q2_ref[...], zero)
    qs = (q1[0], q2[0], q1[1], q2[1])
    k_refs = (k1_ref, k2_ref, k1_ref, k2_ref)

    def scores_fn(c, kb):
        return k_refs[c][pl.ds(pl.multiple_of(kb * t, t), t), :], qs[c]

    def vt_fn(c, kb):
        return vt_ref[0, kb, (c // 2) * DV_A:(c // 2 + 1) * DV_A, :]

    def finalize(ls):
        lp = lam_ref[...]
        lam = (jnp.exp(jnp.sum(lp[0:1] * lp[1:2], axis=1, keepdims=True))
               - jnp.exp(jnp.sum(lp[2:3] * lp[3:4], axis=1, keepdims=True)) + lam_init)
        for e in range(2):
            ot = acc_refs[2 * e][...] / ls[2 * e] - lam * (acc_refs[2 * e + 1][...] / ls[2 * e + 1])
            o = ot.T
            y = o * lax.rsqrt(jnp.mean(o * o, axis=-1, keepdims=True) + EPS)
            y = (y * g_ref[...]) * (1.0 - lam_init)
            o_ref[:, e * DV_A:(e + 1) * DV_A] = y.astype(o_ref.dtype)

    _flash_pipeline(i, t, 4, scores_fn, vt_fn, buf_a, buf_b, acc_refs, finalize)


def _attn_a_sample_kernel(lam_ref, g_ref, q1_ref, q2_ref, k1n_ref, k2n_ref, vn_ref,
                          k1p_ref, k2p_ref, vp_ref, o_ref, m_sc, l_sc, acc_sc, *, lam_init):
    ts = q1_ref.shape[0]
    zero = jnp.zeros((ts, LANES), BF16)
    q1 = _split_heads(q1_ref[...], zero)
    q2 = _split_heads(q2_ref[...], zero)
    _init_state(m_sc, l_sc, acc_sc)
    for k1_ref_, k2_ref_, v_ref_ in ((k1p_ref, k2p_ref, vp_ref), (k1n_ref, k2n_ref, vn_ref)):
        k1 = k1_ref_[...].reshape(k1_ref_.shape[-2:]).astype(BF16)
        k2 = k2_ref_[...].reshape(k2_ref_.shape[-2:]).astype(BF16)
        v = v_ref_[...].reshape(v_ref_.shape[-2:]).astype(BF16)
        for e in range(2):
            ve = v[:, e * DV_A:(e + 1) * DV_A]
            for mp, (q, k) in enumerate(((q1[e], k1), (q2[e], k2))):
                _softmax_step(_qk(q, k), ve, 2 * e + mp, m_sc, l_sc, acc_sc)
    _diff_finalize(lam_ref, g_ref, o_ref, l_sc, acc_sc, lam_init)


def _attn_b_prompt_kernel(qn_ref, qp_ref, kn_ref, kp_ref, vt_ref, o_ref, *scratch, t):
    i = pl.program_id(1)
    buf_a, buf_b, accs = (scratch[0:2], scratch[4:6]), (scratch[2:4], scratch[6:8]), scratch[8:10]
    qn, qp = qn_ref[...], qp_ref[...]
    qs = tuple(jnp.concatenate([qn[:, c * D_NOPE:(c + 1) * D_NOPE], qp[:, c * LANES:(c + 1) * LANES]],
                               axis=1) for c in range(2))

    def scores_fn(c, kb):
        off = pl.multiple_of(kb * t, t)
        k = jnp.concatenate([kn_ref[pl.ds(off, t), c * D_NOPE:(c + 1) * D_NOPE],
                             kp_ref[pl.ds(off, t), :]], axis=1)
        return k, qs[c]

    def vt_fn(c, kb):
        return vt_ref[0, kb, c * DV_B:(c + 1) * DV_B, :]

    def finalize(ls):
        for c in range(2):
            o_ref[:, c * DV_B:(c + 1) * DV_B] = (accs[c][...] / ls[c]).T.astype(o_ref.dtype)

    _flash_pipeline(i, t, 2, scores_fn, vt_fn, buf_a, buf_b, accs, finalize)


def _attn_b_sample_kernel(qn_ref, qp_ref, knn_ref, kpn_ref, vn_ref, knp_ref, kpp_ref, vp_ref,
                          o_ref, m_sc, l_sc, acc_sc):
    q = jnp.concatenate([qn_ref[...], qp_ref[...]], axis=1)
    _init_state(m_sc, l_sc, acc_sc)
    for kn_ref, kp_ref, v_ref in ((knp_ref, kpp_ref, vp_ref), (knn_ref, kpn_ref, vn_ref)):
        kn = kn_ref[...].reshape(kn_ref.shape[-2:])
        kp = kp_ref[...].reshape(kp_ref.shape[-2:])
        v = v_ref[...].reshape(v_ref.shape[-2:])
        _softmax_step(_qk(q, jnp.concatenate([kn, kp], axis=1)), v, 0, m_sc, l_sc, acc_sc)
    o_ref[...] = (acc_sc[0] / l_sc[0]).astype(o_ref.dtype)


def _kv_major_t(v, groups, nblk, t):
    w = v.shape[1] // groups
    return v.reshape(nblk, t, groups, w).transpose(2, 0, 3, 1)


def _attn_scratch(n, t, dv):
    return [pltpu.VMEM((n, t, 1), F32), pltpu.VMEM((n, t, 1), F32), pltpu.VMEM((n, t, dv), F32)]


def _merge_kernel(oa_ref, ob_ref, wa_ref, wb_ref, g0_ref, g1_ref, o_ref):
    ya = jnp.dot(oa_ref[...], wa_ref[...], preferred_element_type=F32)
    yb = jnp.dot(ob_ref[...], wb_ref[...], preferred_element_type=F32)
    o_ref[...] = (g0_ref[...].astype(F32) * ya + g1_ref[...].astype(F32) * yb).astype(o_ref.dtype)


def _conv_gate(u, p0, p1, wc_ref, bc_ref):
    row = lax.broadcasted_iota(jnp.int32, u.shape, 0)
    s1 = jnp.where(row == 0, p1, pltpu.roll(u, 1, axis=0))
    s2 = jnp.where(row == 0, p0, jnp.where(row == 1, p1, pltpu.roll(u, 2, axis=0)))
    wc = wc_ref[...]
    return bc_ref[...] + ((wc[0:1] * s2 + wc[1:2] * s1) + wc[2:3] * u)


def _ffn_up_kernel(x_ref, wg_ref, wu_ref, wc_ref, bc_ref, st_ref, o_ref, tail_ref, carry_ref,
                   *, n_prompt_tiles, tn):
    i = pl.program_id(0)
    j = pl.program_id(1)
    x = x_ref[...]
    ug = jnp.dot(x, wg_ref[...], preferred_element_type=F32)
    uu = jnp.dot(x, wu_ref[...], preferred_element_type=F32)
    tm = ug.shape[0]
    nseg = tm // CHUNK
    col = pl.multiple_of(j * tn, tn)
    for s in range(nseg):
        tail_ref[0, 8 * s:8 * s + 8, :] = ug[CHUNK * s + CHUNK - 8:CHUNK * s + CHUNK]

    @pl.when(i < n_prompt_tiles)
    def _():
        prev = carry_ref[:, pl.ds(col, tn)]
        prev = jnp.where(i == 0, jnp.zeros_like(prev), prev)
        c = _conv_gate(ug, prev[6:7], prev[7:8], wc_ref, bc_ref)
        o_ref[...] = (jax.nn.silu(c) * uu).astype(o_ref.dtype)
        carry_ref[:, pl.ds(col, tn)] = ug[tm - 8:tm]

    @pl.when(i >= n_prompt_tiles)
    def _():
        st = st_ref[...]
        for s in range(nseg):
            seg = ug[CHUNK * s:CHUNK * (s + 1)]
            c = _conv_gate(seg, st[s, 0:1], st[s, 1:2], wc_ref, bc_ref)
            o_ref[CHUNK * s:CHUNK * (s + 1), :] = (
                jax.nn.silu(c) * uu[CHUNK * s:CHUNK * (s + 1)]).astype(o_ref.dtype)


def kernel(x_prompt, x_sample, cache_dk, cache_dv, cache_ckv, cache_kpe, state_conv, g_attn, w_in, lam_q1, lam_k1, lam_q2, lam_k2, g_subln, w_branch_a, g_qa, w_qb, g_kva, w_kvb, w_branch_b, b_gate, w_out, g_ffn, w_ff_gate, w_ff_up, w_conv, b_conv, w_ff_down, g_final):
    depth = w_in.shape[0]
    assert depth == 1
    lam_init = 0.8 - 0.6 * math.exp(-0.3 * 0)
    bp, tp, d = x_prompt.shape
    bs, ts, _ = x_sample.shape
    past = cache_dk.shape[2]
    dff = w_ff_gate.shape[2]
    assert bp == 1 and ts == CHUNK and tp % ROW_TILE == 0 and bs * ts == ROW_TILE
    mp_, ms_ = tp, bs * ts
    m = mp_ + ms_
    tm = ROW_TILE
    n_qk = 2 * H_A * DH_A
    n_va = H_A * DV_A
    o_cq = 2 * n_qk + n_va
    o_ckv = o_cq + Q_LORA
    o_gate = o_ckv + KV_LORA + D_ROPE

    w_in0 = w_in[0]
    w_q = w_in0[:, :n_qk].astype(BF16)
    w_k = w_in0[:, n_qk:2 * n_qk].astype(BF16)
    w_v = w_in0[:, 2 * n_qk:o_cq].astype(BF16)
    w_cq = w_in0[:, o_cq:o_ckv].astype(BF16)
    w_ckv = jnp.pad(w_in0[:, o_ckv:o_gate], ((0, 0), (0, LANES - D_ROPE))).astype(BF16)
    w_gate = w_in0[:, o_gate:].astype(BF16)
    wqb = w_qb[0].reshape(Q_LORA, H_B, D_NOPE + D_ROPE)
    w_qb_nope = wqb[:, :, :D_NOPE].reshape(Q_LORA, H_B * D_NOPE).astype(BF16)
    w_qb_pe = jnp.pad(wqb[:, :, D_NOPE:], ((0, 0), (0, 0), (0, LANES - D_ROPE))
                      ).reshape(Q_LORA, H_B * LANES).astype(BF16)
    w_kvb_p = w_kvb[0].reshape(KV_LORA, H_B, 2, D_NOPE).transpose(0, 2, 1, 3
                                                                  ).reshape(KV_LORA, 2 * H_B * D_NOPE).astype(BF16)
    w_a = w_branch_a[0].astype(BF16)
    w_b = w_branch_b[0].astype(BF16)
    w_o = w_out[0].astype(BF16)
    w_fg = w_ff_gate[0].astype(BF16)
    w_fu = w_ff_up[0].astype(BF16)
    w_fd = w_ff_down[0].astype(BF16)

    x_all = jnp.concatenate([x_prompt.reshape(mp_, d), x_sample.reshape(ms_, d)], axis=0)
    pos = jnp.concatenate([jnp.arange(tp), jnp.tile(past + jnp.arange(ts), bs)])
    tab_a = _rope_tables(pos, ROT_A, ROT_A // 2)
    tab_b = _rope_tables(pos, D_ROPE, D_ROPE // 2)
    tab_spec = (tm, LANES), lambda i, j: (i, 0)

    xn = _rmsnorm(x_all, g_attn[0], BF16)

    def ep_q(acc, ex, outs):
        outs[0][...] = (_rope(acc, *ex, ROT_A // 2) * (SCALE_A * LOG2E)).astype(BF16)

    def ep_k(acc, ex, outs):
        r = _rope(acc, *ex, ROT_A // 2)
        outs[0][...] = r
        outs[1][...] = r.astype(BF16)

    def ep_v(acc, ex, outs):
        outs[0][...] = acc
        outs[1][...] = acc.astype(BF16)

    tn = 512
    blk = (tm, tn), lambda i, j: (i, j)
    tabs_a = [(t_, *tab_spec) for t_ in tab_a]
    tabs_b = [(t_, *tab_spec) for t_ in tab_b]
    (q_a,) = _matmul(xn, w_q, ep_q, tm=tm, tn=tn, tk=d, extras=tabs_a,
                     out_shapes=[jax.ShapeDtypeStruct((m, n_qk), BF16)], out_blocks=[blk], name="proj_q")
    k_f, k_a = _matmul(xn, w_k, ep_k, tm=tm, tn=tn, tk=d, extras=tabs_a,
                       out_shapes=[jax.ShapeDtypeStruct((m, n_qk), F32),
                                   jax.ShapeDtypeStruct((m, n_qk), BF16)],
                       out_blocks=[blk, blk], name="proj_k")
    v_f, v_a = _matmul(xn, w_v, ep_v, tm=tm, tn=tn, tk=d, extras=[],
                       out_shapes=[jax.ShapeDtypeStruct((m, n_va), F32),
                                   jax.ShapeDtypeStruct((m, n_va), BF16)],
                       out_blocks=[blk, blk], name="proj_v")

    def ep_norm(acc, ex, outs):
        y = acc * lax.rsqrt(jnp.mean(acc * acc, axis=-1, keepdims=True) + EPS)
        outs[0][...] = (y * ex[0][...]).astype(BF16)

    (cqn,) = _matmul(xn, w_cq, ep_norm, tm=tm, tn=Q_LORA, tk=d,
                     extras=[(g_qa[0].reshape(1, Q_LORA), (1, Q_LORA), lambda i, j: (0, 0))],
                     out_shapes=[jax.ShapeDtypeStruct((m, Q_LORA), BF16)],
                     out_blocks=[((tm, Q_LORA), lambda i, j: (i, 0))], name="proj_cq")

    def ep_ckv(acc, ex, outs):
        c = acc[:, :KV_LORA]
        y = c * lax.rsqrt(jnp.mean(c * c, axis=-1, keepdims=True) + EPS) * ex[0][...]
        outs[0][...] = y
        outs[1][...] = y.astype(BF16)
        r = _rope(acc[:, KV_LORA:], *ex[1:], D_ROPE // 2)
        outs[2][...] = r
        outs[3][...] = r.astype(BF16)

    n_ckv = KV_LORA + LANES
    ckv_f, ckv_b, kpe_f, kpe_b = _matmul(
        xn, w_ckv, ep_ckv, tm=tm, tn=n_ckv, tk=d,
        extras=[(g_kva[0].reshape(1, KV_LORA), (1, KV_LORA), lambda i, j: (0, 0))] + tabs_b,
        out_shapes=[jax.ShapeDtypeStruct((m, KV_LORA), F32), jax.ShapeDtypeStruct((m, KV_LORA), BF16),
                    jax.ShapeDtypeStruct((m, LANES), F32), jax.ShapeDtypeStruct((m, LANES), BF16)],
        out_blocks=[((tm, KV_LORA), lambda i, j: (i, 0))] * 2 + [((tm, LANES), lambda i, j: (i, 0))] * 2,
        name="proj_ckv")

    def ep_gate(acc, ex, outs):
        outs[0][...] = jax.nn.sigmoid(acc + ex[0][...]).astype(BF16)

    (gate,) = _matmul(xn, w_gate, ep_gate, tm=tm, tn=tn, tk=d,
                      extras=[(b_gate[0].reshape(1, 2 * d), (1, tn), lambda i, j: (0, j))],
                      out_shapes=[jax.ShapeDtypeStruct((m, 2 * d), BF16)], out_blocks=[blk], name="proj_gate")

    def ep_scale(acc, ex, outs):
        outs[0][...] = (acc * (SCALE_B * LOG2E)).astype(BF16)

    def ep_rope_scale(acc, ex, outs):
        outs[0][...] = (_rope(acc, *ex, D_ROPE // 2) * (SCALE_B * LOG2E)).astype(BF16)

    def ep_cast(acc, ex, outs):
        outs[0][...] = acc.astype(BF16)

    nb = H_B * D_NOPE
    (qb_nope,) = _matmul(cqn, w_qb_nope, ep_scale, tm=tm, tn=tn, tk=Q_LORA, extras=[],
                         out_shapes=[jax.ShapeDtypeStruct((m, nb), BF16)], out_blocks=[blk], name="qb_nope")
    (qb_pe,) = _matmul(cqn, w_qb_pe, ep_rope_scale, tm=tm, tn=tn, tk=Q_LORA, extras=tabs_b,
                       out_shapes=[jax.ShapeDtypeStruct((m, nb), BF16)], out_blocks=[blk], name="qb_pe")
    (kvb,) = _matmul(ckv_b, w_kvb_p, ep_cast, tm=tm, tn=tn, tk=KV_LORA, extras=[],
                     out_shapes=[jax.ShapeDtypeStruct((m, 2 * nb), BF16)], out_blocks=[blk], name="kvb_new")
    ckv_past = cache_ckv[0].reshape(bs * past, KV_LORA)
    (kvb_past,) = _matmul(ckv_past, w_kvb_p, ep_cast, tm=tm, tn=tn, tk=KV_LORA, extras=[],
                          out_shapes=[jax.ShapeDtypeStruct((bs * past, 2 * nb), BF16)],
                          out_blocks=[blk], name="kvb_past")

    lam_p = jnp.stack([lam_q1[0], lam_k1[0], lam_q2[0], lam_k2[0]]).astype(F32)
    g_sub = g_subln[0].reshape(1, DV_A).astype(F32)
    npair = H_A // 2
    nq = tp // tm
    const2 = lambda *_: (0, 0)
    oa_p = pl.pallas_call(
        functools.partial(_attn_a_prompt_kernel, t=tm, lam_init=lam_init),
        out_shape=jax.ShapeDtypeStruct((mp_, n_va), BF16),
        grid=(npair, nq),
        in_specs=[pl.BlockSpec((4, DH_A), const2), pl.BlockSpec((1, DV_A), const2),
                  pl.BlockSpec((tm, LANES), lambda p, i: (i, p)),
                  pl.BlockSpec((tm, LANES), lambda p, i: (i, npair + p)),
                  pl.BlockSpec((tp, LANES), lambda p, i: (0, p)),
                  pl.BlockSpec((tp, LANES), lambda p, i: (0, npair + p)),
                  pl.BlockSpec((1, nq, 2 * DV_A, tm), lambda p, i: (p, 0, 0, 0))],
        out_specs=pl.BlockSpec((tm, 2 * DV_A), lambda p, i: (i, p)),
        scratch_shapes=([pltpu.VMEM((tm, tm), F32)] * 8 + [pltpu.VMEM((tm, tm), BF16)] * 8
                        + [pltpu.VMEM((DV_A, tm), F32)] * 4),
        compiler_params=_params(("parallel", "arbitrary")),
        name="attn_a_prompt",
    )(lam_p, g_sub, q_a, q_a, k_a, k_a, _kv_major_t(v_a[:mp_], npair, nq, tm))

    dk_past = cache_dk[0].reshape(bs, past, n_qk)
    dv_past = cache_dv[0].reshape(bs, past, n_va)
    r0 = mp_ // ts
    oa_s = pl.pallas_call(
        functools.partial(_attn_a_sample_kernel, lam_init=lam_init),
        out_shape=jax.ShapeDtypeStruct((ms_, n_va), BF16),
        grid=(bs, npair),
        in_specs=[pl.BlockSpec((4, DH_A), const2), pl.BlockSpec((1, DV_A), const2),
                  pl.BlockSpec((ts, LANES), lambda b, p: (r0 + b, p)),
                  pl.BlockSpec((ts, LANES), lambda b, p: (r0 + b, npair + p)),
                  pl.BlockSpec((ts, LANES), lambda b, p: (r0 + b, p)),
                  pl.BlockSpec((ts, LANES), lambda b, p: (r0 + b, npair + p)),
                  pl.BlockSpec((ts, 2 * DV_A), lambda b, p: (r0 + b, p)),
                  pl.BlockSpec((1, past, LANES), lambda b, p: (b, 0, p)),
                  pl.BlockSpec((1, past, LANES), lambda b, p: (b, 0, npair + p)),
                  pl.BlockSpec((1, past, 2 * DV_A), lambda b, p: (b, 0, p))],
        out_specs=pl.BlockSpec((ts, 2 * DV_A), lambda b, p: (b, p)),
        scratch_shapes=_attn_scratch(4, ts, DV_A),
        compiler_params=_params(("parallel", "arbitrary")),
        name="attn_a_sample",
    )(lam_p, g_sub, q_a, q_a, k_a, k_a, v_a, dk_past, dk_past, dv_past)

    ob_p = pl.pallas_call(
        functools.partial(_attn_b_prompt_kernel, t=tm),
        out_shape=jax.ShapeDtypeStruct((mp_, nb), BF16),
        grid=(H_B // 2, nq),
        in_specs=[pl.BlockSpec((tm, 2 * D_NOPE), lambda h, i: (i, h)),
                  pl.BlockSpec((tm, 2 * LANES), lambda h, i: (i, h)),
                  pl.BlockSpec((tp, 2 * D_NOPE), lambda h, i: (0, h)),
                  pl.BlockSpec((tp, LANES), lambda h, i: (0, 0)),
                  pl.BlockSpec((1, nq, 2 * DV_B, tm), lambda h, i: (h, 0, 0, 0))],
        out_specs=pl.BlockSpec((tm, 2 * DV_B), lambda h, i: (i, h)),
        scratch_shapes=([pltpu.VMEM((tm, tm), F32)] * 4 + [pltpu.VMEM((tm, tm), BF16)] * 4
                        + [pltpu.VMEM((DV_B, tm), F32)] * 2),
        compiler_params=_params(("parallel", "arbitrary")),
        name="attn_b_prompt",
    )(qb_nope, qb_pe, kvb, kpe_b, _kv_major_t(kvb[:mp_, nb:], H_B // 2, nq, tm))

    kpe_past = jnp.pad(cache_kpe[0], ((0, 0), (0, 0), (0, LANES - D_ROPE))).astype(BF16)
    kvb_past3 = kvb_past.reshape(bs, past, 2 * nb)
    ob_s = pl.pallas_call(
        _attn_b_sample_kernel,
        out_shape=jax.ShapeDtypeStruct((ms_, nb), BF16),
        grid=(bs, H_B),
        in_specs=[pl.BlockSpec((ts, D_NOPE), lambda b, h: (r0 + b, h)),
                  pl.BlockSpec((ts, LANES), lambda b, h: (r0 + b, h)),
                  pl.BlockSpec((ts, D_NOPE), lambda b, h: (r0 + b, h)),
                  pl.BlockSpec((ts, LANES), lambda b, h: (r0 + b, 0)),
                  pl.BlockSpec((ts, DV_B), lambda b, h: (r0 + b, H_B + h)),
                  pl.BlockSpec((1, past, D_NOPE), lambda b, h: (b, 0, h)),
                  pl.BlockSpec((1, past, LANES), lambda b, h: (b, 0, 0)),
                  pl.BlockSpec((1, past, DV_B), lambda b, h: (b, 0, H_B + h))],
        out_specs=pl.BlockSpec((ts, DV_B), lambda b, h: (b, h)),
        scratch_shapes=_attn_scratch(1, ts, DV_B),
        compiler_params=_params(("parallel", "arbitrary")),
        name="attn_b_sample",
    )(qb_nope, qb_pe, kvb, kpe_b, kvb, kvb_past3, kpe_past, kvb_past3)

    oa = jnp.concatenate([oa_p, oa_s], axis=0)
    ob = jnp.concatenate([ob_p, ob_s], axis=0)

    ncol = d // tn
    merged = pl.pallas_call(
        _merge_kernel,
        out_shape=jax.ShapeDtypeStruct((m, d), BF16),
        grid=(m // tm, ncol),
        in_specs=[pl.BlockSpec((tm, n_va), lambda i, j: (i, 0)),
                  pl.BlockSpec((tm, nb), lambda i, j: (i, 0)),
                  pl.BlockSpec((n_va, tn), lambda i, j: (0, j)),
                  pl.BlockSpec((nb, tn), lambda i, j: (0, j)),
                  pl.BlockSpec((tm, tn), lambda i, j: (i, j)),
                  pl.BlockSpec((tm, tn), lambda i, j: (i, ncol + j))],
        out_specs=pl.BlockSpec((tm, tn), lambda i, j: (i, j)),
        compiler_params=_params(("parallel", "parallel")),
        name="branch_merge",
    )(oa, ob, w_a, w_b, gate, gate)

    def ep_resid(acc, ex, outs):
        outs[0][...] = ex[0][...] + acc

    (h,) = _matmul(merged, w_o, ep_resid, tm=tm, tn=tn, tk=d, extras=[(x_all, *blk)],
                   out_shapes=[jax.ShapeDtypeStruct((m, d), F32)], out_blocks=[blk], name="out_proj")

    hn = _rmsnorm(h, g_ffn[0], BF16)
    tnf = 256
    n_pt = mp_ // tm
    nseg = tm // CHUNK
    act, tails = pl.pallas_call(
        functools.partial(_ffn_up_kernel, n_prompt_tiles=n_pt, tn=tnf),
        out_shape=[jax.ShapeDtypeStruct((m, dff), BF16),
                   jax.ShapeDtypeStruct((m // tm, 8 * nseg, dff), F32)],
        grid=(m // tm, dff // tnf),
        in_specs=[pl.BlockSpec((tm, d), lambda i, j: (i, 0)),
                  pl.BlockSpec((d, tnf), lambda i, j: (0, j)),
                  pl.BlockSpec((d, tnf), lambda i, j: (0, j)),
                  pl.BlockSpec((CONV_W, tnf), lambda i, j: (0, j)),
                  pl.BlockSpec((1, tnf), lambda i, j: (0, j)),
                  pl.BlockSpec((bs, CONV_W - 1, tnf), lambda i, j: (0, 0, j))],
        out_specs=[pl.BlockSpec((tm, tnf), lambda i, j: (i, j)),
                   pl.BlockSpec((1, 8 * nseg, tnf), lambda i, j: (i, 0, j))],
        scratch_shapes=[pltpu.VMEM((8, dff), F32)],
        compiler_params=_params(("arbitrary", "arbitrary")),
        name="ffn_up",
    )(hn, w_fg, w_fu, w_conv[0], b_conv[0].reshape(1, dff), state_conv[0])

    tkd = dff // 2
    (y_pre,) = _matmul(act, w_fd, ep_resid, tm=tm, tn=tn, tk=tkd, extras=[(h, *blk)],
                       out_shapes=[jax.ShapeDtypeStruct((m, d), F32)], out_blocks=[blk], name="ffn_down")
    y = _rmsnorm(y_pre, g_final, F32)

    def split(a, tail):
        return a[:mp_].reshape(1, bp, tp, *tail), a[mp_:].reshape(1, bs, ts, *tail)

    dk_p, dk_s = split(k_f, (2 * H_A, DH_A))
    dv_p, dv_s = split(v_f, (H_A, DV_A))
    ckv_p, ckv_s = split(ckv_f, (KV_LORA,))
    kpe_p, kpe_s = split(kpe_f[:, :D_ROPE], (D_ROPE,))
    conv_p = tails[n_pt - 1, 8 * nseg - 2:].reshape(1, bp, CONV_W - 1, dff)
    conv_s = tails[n_pt].reshape(nseg, 8, dff)[:, 6:].reshape(1, bs, CONV_W - 1, dff)
    y_p = y[:mp_].reshape(bp, tp, d)
    y_s = y[mp_:].reshape(bs, ts, d)
    return (y_p, y_s, dk_p, dv_p, ckv_p, kpe_p, conv_p, dk_s, dv_s, ckv_s, kpe_s, conv_s)
```
